```python
import math
import jax, jax.numpy as jnp
from jax import lax
import numpy as np

D_MODEL = 1024
BATCH = 2
SEQ = 8192
DEPTH = 4
DEC_BATCH = 32
DEC_SEQ = 8
PAST_LEN = 8192
PAGE_SIZE = 128

HEAD_DIM = 64
W_SB = D_MODEL // 2
W_NSA = D_MODEL - W_SB
H_SB = W_SB // HEAD_DIM
H_NSA = W_NSA // HEAD_DIM
G_NSA = 2
R_NSA = H_NSA // G_NSA
KV_NSA = G_NSA * HEAD_DIM
CMP_LEN = 32
CMP_STRIDE = 16
SEL_BLOCK = 64
SEL_TOPK = 16
WINDOW = 512
Q_BLOCK = 128
N_BUCKETS = 32
REL_MAX_EXACT = N_BUCKETS // 2
REL_MAX_DIST = 1024
ALPHA = (2 * DEPTH) ** 0.25
BETA_INIT = (8 * DEPTH) ** -0.25
LN_EPS = 1e-5
RMS_EPS = 1e-6
NEG = -1e30
FORCE_SCORE = 1e4

kernel_name = 'hybrid_stickbreak_nsa_decoder_step'


def _split_sizes():
    return (W_SB, W_SB, W_SB, W_SB, W_NSA, 6 * KV_NSA, 3 * H_NSA, W_NSA)


def layer_norm(x, g, b):
    xf = x.astype(jnp.float32)
    mu = jnp.mean(xf, axis=-1, keepdims=True)
    var = jnp.mean(jnp.square(xf - mu), axis=-1, keepdims=True)
    return ((xf - mu) * lax.rsqrt(var + LN_EPS) * g + b).astype(x.dtype)


def rms_norm(x, g):
    xf = x.astype(jnp.float32)
    ms = jnp.mean(jnp.square(xf), axis=-1, keepdims=True)
    return (xf * lax.rsqrt(ms + RMS_EPS) * g).astype(x.dtype)


def rel_bucket(dist):
    n = jnp.maximum(dist, 0)
    nf = jnp.maximum(n, 1).astype(jnp.float32)
    large = REL_MAX_EXACT + (jnp.log(nf / REL_MAX_EXACT) / math.log(REL_MAX_DIST / REL_MAX_EXACT)
                             * (N_BUCKETS - REL_MAX_EXACT)).astype(jnp.int32)
    large = jnp.minimum(large, N_BUCKETS - 1)
    return jnp.where(n < REL_MAX_EXACT, n, large)


def group_bias(rel_bias, bucket):
    b = rel_bias[bucket].reshape(bucket.shape + (G_NSA, R_NSA))
    return jnp.moveaxis(b, (-2, -1), (0, 1))


def masked_softmax(s, mask):
    s = jnp.where(mask, s.astype(jnp.float32), NEG)
    p = jax.nn.softmax(s, axis=-1)
    return jnp.where(mask, p, 0.0)


def stick_breaking(q, q_pos, k, v, k_pos):
    z = jnp.einsum('bqhd,bthd->bhqt', q, k).astype(jnp.float32) * (HEAD_DIM ** -0.5)
    mask = k_pos[None, :] < q_pos[:, None]
    log_keep = jnp.where(mask, jax.nn.log_sigmoid(-z), 0.0)
    after = lax.cumsum(log_keep, axis=3, reverse=True) - log_keep
    a = jnp.where(mask, jnp.exp(jax.nn.log_sigmoid(z) + after), 0.0)
    return jnp.einsum('bhqt,bthd->bqhd', a.astype(v.dtype), v)


def compress_kv(kv_cmp, w1, w2, pos):
    t = kv_cmp.shape[1]
    n_c = (t - CMP_LEN) // CMP_STRIDE + 1
    idx = jnp.arange(n_c)[:, None] * CMP_STRIDE + jnp.arange(CMP_LEN)[None, :]
    blocks = kv_cmp[:, idx] + jnp.transpose(pos, (1, 0, 2))[:, :, None, :]
    hid = jax.nn.gelu(jnp.einsum('bnlcgd,cldh->bncgh', blocks, w1))
    out = jnp.einsum('bncgh,che->bncge', hid, w2)
    c_end = jnp.arange(n_c) * CMP_STRIDE + CMP_LEN - 1
    return out[:, :, 0], out[:, :, 1], c_end


def sel_blocks(kv_sel):
    bsz, t = kv_sel.shape[0], kv_sel.shape[1]
    n_blk = -(-t // SEL_BLOCK)
    x = jnp.pad(kv_sel, ((0, 0), (0, n_blk * SEL_BLOCK - t), (0, 0), (0, 0), (0, 0)))
    x = x.reshape(bsz, n_blk, SEL_BLOCK, 2, G_NSA, HEAD_DIM).transpose(3, 0, 4, 1, 2, 5)
    return x[0], x[1]


def selection_map(n_c, n_blk):
    cs = jnp.arange(n_c)[:, None] * CMP_STRIDE
    ss = jnp.arange(n_blk)[None, :] * SEL_BLOCK
    ov = jnp.minimum(cs + CMP_LEN, ss + SEL_BLOCK) - jnp.maximum(cs, ss)
    return jnp.clip(ov, 0, None).astype(jnp.float32) / CMP_LEN


def nsa_attend(q, q_pos, k_c, v_c, c_end, k_sel, v_sel, win_kv, win_pos, rel_bias):
    bsz, nq = q.shape[0], q.shape[1]
    scale = HEAD_DIM ** -0.5
    qg = q.reshape(bsz, nq, G_NSA, R_NSA, HEAD_DIM)
    s_c = jnp.einsum('bqgrd,bngd->bgrqn', qg, k_c).astype(jnp.float32) * scale
    s_c = s_c + group_bias(rel_bias, rel_bucket(q_pos[:, None] - c_end[None, :]))
    p_c = masked_softmax(s_c, c_end[None, :] <= q_pos[:, None])
    o_c = jnp.einsum('bgrqn,bngd->bqgrd', p_c.astype(v_c.dtype), v_c)
    n_blk = k_sel.shape[2]
    imp = jnp.einsum('bgrqn,nj->bgqj', p_c, selection_map(k_c.shape[1], n_blk))
    blk = jnp.arange(n_blk)[None, :]
    cur = (q_pos // SEL_BLOCK)[:, None]
    forced = (blk == 0) | (blk == cur) | (blk == cur - 1)
    valid = blk * SEL_BLOCK <= q_pos[:, None]
    score = jnp.where(forced, FORCE_SCORE, jnp.where(valid, imp, -1.0))
    _, idx = lax.top_k(score, min(SEL_TOPK, n_blk))
    b_ix = jnp.arange(bsz)[:, None, None, None]
    g_ix = jnp.arange(G_NSA)[None, :, None, None]
    k_s = k_sel[b_ix, g_ix, idx]
    v_s = v_sel[b_ix, g_ix, idx]
    tok = idx[..., None] * SEL_BLOCK + jnp.arange(SEL_BLOCK)
    dist_s = q_pos[None, None, :, None, None] - tok
    rb = rel_bias.reshape(N_BUCKETS, G_NSA, R_NSA).transpose(1, 0, 2)
    bias_s = jnp.moveaxis(rb[g_ix[..., None], rel_bucket(dist_s)], -1, 2)
    s_s = jnp.einsum('bqgrd,bgqnkd->bgrqnk', qg, k_s).astype(jnp.float32) * scale + bias_s
    n_keys = idx.shape[-1] * SEL_BLOCK
    p_s = masked_softmax(s_s.reshape(bsz, G_NSA, R_NSA, nq, n_keys),
                         (dist_s >= 0).reshape(bsz, G_NSA, 1, nq, n_keys))
    o_s = jnp.einsum('bgrqnk,bgqnkd->bqgrd', p_s.reshape(s_s.shape).astype(v_s.dtype), v_s)
    k_w, v_w = win_kv[:, :, 0], win_kv[:, :, 1]
    dist_w = q_pos[:, None] - win_pos[None, :]
    m_w = (win_pos[None, :] >= 0) & (dist_w >= 0) & (dist_w < WINDOW)
    s_w = jnp.einsum('bqgrd,bkgd->bgrqk', qg, k_w).astype(jnp.float32) * scale
    s_w = s_w + group_bias(rel_bias, rel_bucket(dist_w))
    p_w = masked_softmax(s_w, m_w)
    o_w = jnp.einsum('bgrqk,bkgd->bqgrd', p_w.astype(v_w.dtype), v_w)
    shp = (bsz, nq, H_NSA, HEAD_DIM)
    return o_c.reshape(shp), o_s.reshape(shp), o_w.reshape(shp)


def mixer_inputs(y, c, w_ada_l, b_ada_l, w_in_l):
    bsz, t = y.shape[0], y.shape[1]
    mod = jax.nn.silu(c) @ w_ada_l + b_ada_l
    shift, scale, gate = jnp.split(mod, 3, axis=-1)
    h = y * (1.0 + scale[:, None, :]) + shift[:, None, :]
    cuts = np.cumsum(_split_sizes())[:-1].tolist()
    q_a, k_a, v_a, z_a, q_b, kv_b, g_b, z_b = jnp.split(h @ w_in_l, cuts, axis=-1)
    hs = (bsz, t, H_SB, HEAD_DIM)
    return (gate, q_a.reshape(hs), k_a.reshape(hs), v_a.reshape(hs), z_a,
            q_b.reshape(bsz, t, H_NSA, HEAD_DIM), kv_b.reshape(bsz, t, 6, G_NSA, HEAD_DIM),
            g_b.reshape(bsz, t, 3, H_NSA), z_b)


def mixer_output(y, gate, o_a, o_c, o_s, o_w, z_a, g_b, z_b, norm_l, w_out_l, ln_g_l, ln_b_l):
    bsz, t = y.shape[0], y.shape[1]
    gb = jax.nn.sigmoid(g_b)[..., None]
    o_b = gb[:, :, 0] * o_c + gb[:, :, 1] * o_s + gb[:, :, 2] * o_w
    u_a = rms_norm(o_a.reshape(bsz, t, W_SB), norm_l[:W_SB]) * jax.nn.silu(z_a)
    u_b = rms_norm(o_b.reshape(bsz, t, W_NSA), norm_l[W_SB:]) * jax.nn.silu(z_b)
    mixed = jnp.concatenate([u_a, u_b], axis=-1) @ w_out_l
    return layer_norm(ALPHA * y + gate[:, None, :] * mixed, ln_g_l, ln_b_l)


def prompt_mix(q_a, k_a, v_a, q_b, kv_b, w1, w2, pos, rel_bias):
    bsz, seq = q_a.shape[0], q_a.shape[1]
    k_c, v_c, c_end = compress_kv(kv_b[:, :, 0:2], w1, w2, pos)
    k_sel, v_sel = sel_blocks(kv_b[:, :, 2:4])
    win_pad = jnp.pad(kv_b[:, :, 4:6], ((0, 0), (WINDOW, 0), (0, 0), (0, 0), (0, 0)))
    k_pos = jnp.arange(seq)

    def block(i):
        q0 = i * Q_BLOCK
        q_pos = q0 + jnp.arange(Q_BLOCK)
        qa = lax.dynamic_slice_in_dim(q_a, q0, Q_BLOCK, axis=1)
        qb = lax.dynamic_slice_in_dim(q_b, q0, Q_BLOCK, axis=1)
        o_a = stick_breaking(qa, q_pos, k_a, v_a, k_pos)
        w_blk = lax.dynamic_slice_in_dim(win_pad, q0, WINDOW + Q_BLOCK, axis=1)
        w_pos = q0 - WINDOW + jnp.arange(WINDOW + Q_BLOCK)
        o_c, o_s, o_w = nsa_attend(qb, q_pos, k_c, v_c, c_end, k_sel, v_sel, w_blk, w_pos, rel_bias)
        return o_a, o_c, o_s, o_w

    outs = lax.map(block, jnp.arange(seq // Q_BLOCK))
    unblock = lambda o: jnp.moveaxis(o, 0, 1).reshape((bsz, seq) + o.shape[3:])
    return unblock(outs[0]), unblock(outs[1]), unblock(outs[2]), unblock(outs[3])


def sample_mix(q_a, k_a, v_a, q_b, kv_b, past_sb, past_nsa, win_buf, w1, w2, pos, rel_bias):
    past, nq = past_sb.shape[1], q_a.shape[1]
    q_pos = past + jnp.arange(nq)
    k_pos = jnp.arange(past + nq)
    k_all = jnp.concatenate([past_sb[:, :, 0], k_a], axis=1)
    v_all = jnp.concatenate([past_sb[:, :, 1], v_a], axis=1)
    o_a = stick_breaking(q_a, q_pos, k_all, v_all, k_pos)
    nsa_all = jnp.concatenate([past_nsa, kv_b[:, :, 0:4]], axis=1)
    k_c, v_c, c_end = compress_kv(nsa_all[:, :, 0:2], w1, w2, pos)
    k_sel, v_sel = sel_blocks(nsa_all[:, :, 2:4])
    win_all = jnp.concatenate([win_buf, kv_b[:, :, 4:6]], axis=1)
    w_pos = past - win_buf.shape[1] + jnp.arange(win_all.shape[1])
    o_c, o_s, o_w = nsa_attend(q_b, q_pos, k_c, v_c, c_end, k_sel, v_sel, win_all, w_pos, rel_bias)
    keep = min(WINDOW, past + nq)
    new_win = win_all[:, win_all.shape[1] - keep:]
    return o_a, o_c, o_s, o_w, new_win


def setup_inputs(seed: int = 0) -> dict:
    key = jax.random.key(seed)
    ks = jax.random.split(key, 24)
    n_pages = PAST_LEN // PAGE_SIZE
    n_pool = (5 * DEC_BATCH * n_pages) // 4
    w_eff = min(WINDOW, PAST_LEN)
    n_in = sum(_split_sizes())
    nrm = lambda k, shape, s: s * jax.random.normal(k, shape, jnp.float32)
    perm = jax.random.permutation(ks[5], n_pool)
    page_table = perm[: DEC_BATCH * n_pages].reshape(DEC_BATCH, n_pages).astype(jnp.int32)
    d = D_MODEL
    return {
        'x_prompt': nrm(ks[0], (BATCH, SEQ, d), 1.0),
        'x_sample': nrm(ks[1], (DEC_BATCH, DEC_SEQ, d), 1.0),
        'cache_sb_kv': nrm(ks[2], (DEPTH, n_pool, PAGE_SIZE, 2, H_SB, HEAD_DIM), 1.0),
        'cache_nsa_kv': nrm(ks[3], (DEPTH, n_pool, PAGE_SIZE, 4, G_NSA, HEAD_DIM), 1.0),
        'cache_win_kv': nrm(ks[4], (DEPTH, DEC_BATCH, w_eff, 2, G_NSA, HEAD_DIM), 1.0),
        'page_table': page_table,
        'c_prompt': nrm(ks[6], (BATCH, d), 1.0),
        'c_sample': nrm(ks[7], (DEC_BATCH, d), 1.0),
        'ln_in_g': 1.0 + nrm(ks[8], (d,), 0.05),
        'ln_in_b': nrm(ks[9], (d,), 0.05),
        'w_ada': nrm(ks[10], (DEPTH, d, 3 * d), d ** -0.5),
        'b_ada': nrm(ks[11], (DEPTH, 3 * d), 0.02),
        'w_in': nrm(ks[12], (DEPTH, d, n_in), d ** -0.5),
        'w_cmp1': nrm(ks[13], (DEPTH, 2, CMP_LEN, HEAD_DIM, HEAD_DIM), (CMP_LEN * HEAD_DIM) ** -0.5),
        'w_cmp2': nrm(ks[14], (DEPTH, 2, HEAD_DIM, HEAD_DIM), HEAD_DIM ** -0.5),
        'pos_cmp': nrm(ks[15], (DEPTH, 2, CMP_LEN, HEAD_DIM), 0.5),
        'norm_grp': 1.0 + nrm(ks[16], (DEPTH, d), 0.05),
        'w_out': nrm(ks[17], (DEPTH, d, d), BETA_INIT * d ** -0.5),
        'ln_g': 1.0 + nrm(ks[18], (DEPTH, d), 0.05),
        'ln_b': nrm(ks[19], (DEPTH, d), 0.05),
        'rel_bias': nrm(ks[20], (N_BUCKETS, H_NSA), 0.5),
    }


def reference(x_prompt, x_sample, cache_sb_kv, cache_nsa_kv, cache_win_kv, page_table,
              c_prompt, c_sample, ln_in_g, ln_in_b, w_ada, b_ada, w_in, w_cmp1, w_cmp2,
              pos_cmp, norm_grp, w_out, ln_g, ln_b, rel_bias):
    seq = x_prompt.shape[1]
    dec_b = x_sample.shape[0]
    past_len = page_table.shape[1] * PAGE_SIZE
    y_p = layer_norm(x_prompt, ln_in_g, ln_in_b)
    y_s = layer_norm(x_sample, ln_in_g, ln_in_b)
    p_sb, p_nsa, p_win, s_sb, s_nsa, s_win = [], [], [], [], [], []
    for l in range(DEPTH):
        gate, q_a, k_a, v_a, z_a, q_b, kv_b, g_b, z_b = mixer_inputs(y_p, c_prompt, w_ada[l], b_ada[l], w_in[l])
        o_a, o_c, o_s, o_w = prompt_mix(q_a, k_a, v_a, q_b, kv_b, w_cmp1[l], w_cmp2[l], pos_cmp[l], rel_bias)
        p_sb.append(jnp.stack([k_a, v_a], axis=2))
        p_nsa.append(kv_b[:, :, 0:4])
        p_win.append(kv_b[:, seq - min(WINDOW, seq):, 4:6])
        y_p = mixer_output(y_p, gate, o_a, o_c, o_s, o_w, z_a, g_b, z_b,
                           norm_grp[l], w_out[l], ln_g[l], ln_b[l])
        gate, q_a, k_a, v_a, z_a, q_b, kv_b, g_b, z_b = mixer_inputs(y_s, c_sample, w_ada[l], b_ada[l], w_in[l])
        past_sb = cache_sb_kv[l, page_table].reshape(dec_b, past_len, 2, H_SB, HEAD_DIM)
        past_nsa = cache_nsa_kv[l, page_table].reshape(dec_b, past_len, 4, G_NSA, HEAD_DIM)
        o_a, o_c, o_s, o_w, new_win = sample_mix(q_a, k_a, v_a, q_b, kv_b, past_sb, past_nsa,
                                                 cache_win_kv[l], w_cmp1[l], w_cmp2[l], pos_cmp[l], rel_bias)
        s_sb.append(jnp.stack([k_a, v_a], axis=2))
        s_nsa.append(kv_b[:, :, 0:4])
        s_win.append(new_win)
        y_s = mixer_output(y_s, gate, o_a, o_c, o_s, o_w, z_a, g_b, z_b,
                           norm_grp[l], w_out[l], ln_g[l], ln_b[l])
    return (y_p, y_s, jnp.stack(p_sb), jnp.stack(p_nsa), jnp.stack(p_win),
            jnp.stack(s_sb), jnp.stack(s_nsa), jnp.stack(s_win))
```

```python
import functools
import math

import numpy as np
import jax
import jax.numpy as jnp
from jax import lax
from jax.experimental import pallas as pl
from jax.experimental.pallas import tpu as pltpu

F32 = jnp.float32
BF16 = jnp.bfloat16

D_MODEL = 1024
HEAD_DIM = 64
W_SB = D_MODEL // 2
W_NSA = D_MODEL - W_SB
H_SB = W_SB // HEAD_DIM
H_NSA = W_NSA // HEAD_DIM
G_NSA = 2
R_NSA = H_NSA // G_NSA
KV_NSA = G_NSA * HEAD_DIM
CMP_LEN = 32
CMP_STRIDE = 16
SEL_BLOCK = 64
SEL_TOPK = 16
WINDOW = 512
PAGE_SIZE = 128
N_BUCKETS = 32
REL_MAX_EXACT = N_BUCKETS // 2
REL_MAX_DIST = 1024
LN_EPS = 1e-5
RMS_EPS = 1e-6
NEG = -1e30
FORCE_SCORE = 1e4
QK_SCALE = HEAD_DIM ** -0.5

LANES = 128
SUBLANES = 8
VMEM_LIMIT = 56 * 1024 * 1024

C_QA, C_KA, C_VA, C_ZA, C_QB, C_KVB = 0, 512, 1024, 1536, 2048, 2560
C_GATE, C_ZB, N_PROJ = 3328, 3584, 4096
SB_DEAD = -104.0
BIAS_DMAX = 1024
N_TOK_TILES = 9
N_CMP_TILES = 24
TQ = 128


def _bf(x):
    return x.astype(BF16)


def _dot(a, b):
    return jnp.dot(a, b, preferred_element_type=F32)


def _dot_t(a, b):
    return lax.dot_general(a, b, (((1,), (1,)), ((), ())), preferred_element_type=F32)


def _split(x):
    hi = _bf(x)
    lo = _bf(x - hi.astype(F32))
    return hi, lo


def _dot3_t(a, b):
    ah, al = _split(a)
    bh, bl = _split(b)
    return _dot_t(ah, bh) + _dot_t(ah, bl) + _dot_t(al, bh)


def _dot3(a, b):
    ah, al = _split(a)
    bh, bl = _split(b)
    return _dot(ah, bh) + _dot(ah, bl) + _dot(al, bh)


def _dot2_exact_rhs(a, b_bf):
    ah, al = _split(a)
    return _dot(ah, b_bf) + _dot(al, b_bf)


def _softplus(z):
    return jnp.maximum(z, 0.0) + jnp.log1p(jnp.exp(-jnp.abs(z)))


def _cparams(sem):
    return pltpu.CompilerParams(dimension_semantics=sem, vmem_limit_bytes=VMEM_LIMIT)


def _ln_kernel(x_ref, g_ref, b_ref, o_ref):
    x = x_ref[...]
    mu = jnp.mean(x, axis=-1, keepdims=True)
    xc = x - mu
    var = jnp.mean(xc * xc, axis=-1, keepdims=True)
    o_ref[...] = xc * lax.rsqrt(var + LN_EPS) * g_ref[...] + b_ref[...]


def _layer_norm(x2d, g, b):
    rows, d = x2d.shape
    tm = min(512, rows)
    return pl.pallas_call(
        _ln_kernel,
        grid=(rows // tm,),
        in_specs=[pl.BlockSpec((tm, d), lambda i: (i, 0)),
                  pl.BlockSpec((1, d), lambda i: (0, 0)),
                  pl.BlockSpec((1, d), lambda i: (0, 0))],
        out_specs=pl.BlockSpec((tm, d), lambda i: (i, 0)),
        out_shape=jax.ShapeDtypeStruct((rows, d), F32),
        compiler_params=_cparams(("parallel",)),
    )(x2d, g.reshape(1, d), b.reshape(1, d))


def _mod_kernel(c_ref, w_ref, b_ref, o_ref):
    c = c_ref[...]
    s = c * jax.nn.sigmoid(c)
    o_ref[0] = _dot3(s, w_ref[0]) + b_ref[0]


def _ada_mod(c_all, w_ada, b_ada):
    depth, d, n3 = w_ada.shape
    rows = c_all.shape[0]
    tn = 1024
    return pl.pallas_call(
        _mod_kernel,
        grid=(depth, n3 // tn),
        in_specs=[pl.BlockSpec((rows, d), lambda l, j: (0, 0)),
                  pl.BlockSpec((1, d, tn), lambda l, j: (l, 0, j)),
                  pl.BlockSpec((1, 1, tn), lambda l, j: (l, 0, j))],
        out_specs=pl.BlockSpec((1, rows, tn), lambda l, j: (l, 0, j)),
        out_shape=jax.ShapeDtypeStruct((depth, rows, n3), F32),
        compiler_params=_cparams(("parallel", "parallel")),
    )(c_all, w_ada, b_ada.reshape(depth, 1, n3))


def _inproj_kernel(y_ref, sh_ref, sc_ref, w_ref, o_ref):
    h = y_ref[0] * (1.0 + sc_ref[0]) + sh_ref[0]
    hb = _bf(h)
    chunk = 512
    for c0 in range(0, N_PROJ, chunk):
        o_ref[0, :, c0:c0 + chunk] = _dot(hb, w_ref[:, c0:c0 + chunk])


def _in_proj(y, shift, scale, w_bf):
    bsz, t, d = y.shape
    tm = min(256, t)
    tmod = shift.shape[1]
    if tmod == 1:
        mod_spec = pl.BlockSpec((1, 1, d), lambda b, i: (b, 0, 0))
    else:
        mod_spec = pl.BlockSpec((1, tm, d), lambda b, i: (b, i, 0))
    return pl.pallas_call(
        _inproj_kernel,
        grid=(bsz, t // tm),
        in_specs=[pl.BlockSpec((1, tm, d), lambda b, i: (b, i, 0)), mod_spec, mod_spec,
                  pl.BlockSpec((d, N_PROJ), lambda b, i: (0, 0))],
        out_specs=pl.BlockSpec((1, tm, N_PROJ), lambda b, i: (b, i, 0)),
        out_shape=jax.ShapeDtypeStruct((bsz, t, N_PROJ), F32),
        compiler_params=_cparams(("parallel", "parallel")),
    )(y, shift, scale, w_bf)


def _sb_tile(qh, kt, vt, mask, carry, tri):
    z = _dot_t(qh, kt)
    lk = jnp.where(mask, -_softplus(z), 0.0)
    hi, lo = _split(lk)
    after = _dot(hi, tri) + _dot(lo, tri) + carry
    a = jnp.where(mask, jnp.exp(z + lk + after), 0.0)
    contrib = _dot(_bf(a), vt)
    carry = carry + jnp.sum(lk, axis=1, keepdims=True)
    return contrib, carry


def _tri_newer():
    r = lax.broadcasted_iota(jnp.int32, (LANES, LANES), 0)
    c = lax.broadcasted_iota(jnp.int32, (LANES, LANES), 1)
    return jnp.where(r > c, 1.0, 0.0).astype(BF16)


def _sb_prompt_kernel(q_ref, k_ref, v_ref, o_ref, *, tq):
    i = pl.program_id(2)
    q0 = i * tq
    q = q_ref[0] * QK_SCALE
    lane = lax.broadcasted_iota(jnp.int32, (tq, LANES), 1)
    row_pos = q0 + lax.broadcasted_iota(jnp.int32, (tq, LANES), 0)
    tri = _tri_newer()
    outs = []
    for half in range(2):
        in_half = (lane >= HEAD_DIM) if half else (lane < HEAD_DIM)
        qh = _bf(jnp.where(in_half, q, 0.0))

        def cond(c):
            j, alive, _, _ = c
            return jnp.logical_and(j >= 0, alive)

        def body(c):
            j, _, carry, acc = c
            k0 = pl.multiple_of(j * LANES, LANES)
            kt = _bf(k_ref[0, pl.ds(k0, LANES), :])
            vt = _bf(v_ref[0, pl.ds(k0, LANES), :])
            mask = (k0 + lane) < row_pos
            contrib, carry = _sb_tile(qh, kt, vt, mask, carry, tri)
            return j - 1, jnp.max(carry) > SB_DEAD, carry, acc + contrib

        j0 = (q0 + tq - 1) // LANES
        zero = jnp.zeros((tq, LANES), F32)
        _, _, _, acc = lax.while_loop(cond, body, (j0, True, zero, zero))
        outs.append(acc)
    o_ref[0] = jnp.where(lane < HEAD_DIM, outs[0], outs[1])


def _sb_prompt(proj, tq=256):
    bsz, t, _ = proj.shape
    tq = min(tq, t)
    npair = W_SB // LANES
    return pl.pallas_call(
        functools.partial(_sb_prompt_kernel, tq=tq),
        grid=(bsz, npair, t // tq),
        in_specs=[pl.BlockSpec((1, tq, LANES), lambda b, p, i: (b, i, C_QA // LANES + p)),
                  pl.BlockSpec((1, t, LANES), lambda b, p, i: (b, 0, C_KA // LANES + p)),
                  pl.BlockSpec((1, t, LANES), lambda b, p, i: (b, 0, C_VA // LANES + p))],
        out_specs=pl.BlockSpec((1, tq, LANES), lambda b, p, i: (b, i, p)),
        out_shape=jax.ShapeDtypeStruct((bsz, t, W_SB), F32),
        compiler_params=_cparams(("parallel", "parallel", "arbitrary")),
    )(proj, proj, proj)


def _sb_sample_kernel(pt_ref, p_ref, last_ref, cache_ref, o_ref, buf, sem, carry_ref, acc_ref,
                      *, layer_base, n_pages, nq):
    b = pl.program_id(0)
    rows = H_SB * nq
    q = p_ref[0, :, C_QA:C_QA + W_SB] * QK_SCALE
    lane_c = lax.broadcasted_iota(jnp.int32, (rows, W_SB), 1)
    row_c = lax.broadcasted_iota(jnp.int32, (rows, W_SB), 0)
    own = (lane_c // HEAD_DIM) == (row_c // nq)
    qbd = _bf(jnp.where(own, jnp.concatenate([q] * H_SB, axis=0), 0.0))
    tri = _tri_newer()
    key = lax.broadcasted_iota(jnp.int32, (rows, LANES), 1)
    qi = lax.broadcasted_iota(jnp.int32, (rows, LANES), 0) % nq

    pad = jnp.zeros((LANES - nq, W_SB), F32)
    k_new = _bf(jnp.concatenate([p_ref[0, :, C_KA:C_KA + W_SB], pad], axis=0))
    v_new = _bf(jnp.concatenate([p_ref[0, :, C_VA:C_VA + W_SB], pad], axis=0))
    contrib, carry = _sb_tile(qbd, k_new, v_new, key < qi, jnp.zeros((rows, LANES), F32), tri)
    carry_ref[...] = carry
    acc_ref[...] = contrib

    def page_tile(kv_ref):
        kt = _bf(kv_ref[:, 0:W_SB])
        vt = _bf(kv_ref[:, W_SB:2 * W_SB])
        contrib, carry = _sb_tile(qbd, kt, vt, key >= 0, carry_ref[...], tri)
        carry_ref[...] = carry
        acc_ref[...] += contrib

    @pl.when(jnp.max(carry_ref[...]) > SB_DEAD)
    def _():
        page_tile(last_ref.at[0])

    def cond(c):
        j, alive = c
        return jnp.logical_and(j >= 0, alive)

    def body(c):
        j, _ = c
        page = layer_base + pt_ref[b, j]
        cp = pltpu.make_async_copy(cache_ref.at[page], buf, sem)
        cp.start()
        cp.wait()
        page_tile(buf)
        return j - 1, jnp.max(carry_ref[...]) > SB_DEAD

    lax.while_loop(cond, body, (n_pages - 2, jnp.max(carry_ref[...]) > SB_DEAD))

    acc = jnp.where(own, acc_ref[...], 0.0)
    o = acc[0:nq]
    for h in range(1, H_SB):
        o = o + acc[h * nq:(h + 1) * nq]
    o_ref[0] = o


def _sb_sample(proj_s, cache_pages, page_table, layer_base, nq):
    dbsz, n_pages = page_table.shape
    rows = H_SB * nq
    kern = functools.partial(_sb_sample_kernel, layer_base=layer_base, n_pages=n_pages, nq=nq)
    grid_spec = pltpu.PrefetchScalarGridSpec(
        num_scalar_prefetch=1,
        grid=(dbsz,),
        in_specs=[pl.BlockSpec((1, nq, C_QB), lambda b, pt: (0, b, 0)),
                  pl.BlockSpec((1, PAGE_SIZE, 2 * W_SB),
                               lambda b, pt: (layer_base + pt[b, n_pages - 1], 0, 0)),
                  pl.BlockSpec(memory_space=pl.ANY)],
        out_specs=pl.BlockSpec((1, nq, W_SB), lambda b, pt: (0, b, 0)),
        scratch_shapes=[pltpu.VMEM((PAGE_SIZE, 2 * W_SB), F32),
                        pltpu.SemaphoreType.DMA(()),
                        pltpu.VMEM((rows, LANES), F32),
                        pltpu.VMEM((rows, W_SB), F32)],
    )
    return pl.pallas_call(
        kern, grid_spec=grid_spec,
        out_shape=jax.ShapeDtypeStruct((1, dbsz * nq, W_SB), F32),
        compiler_params=_cparams(("arbitrary",)),
    )(page_table, proj_s, cache_pages, cache_pages)


def _compress_rows(x_refs, ncp, w1_ref, pos_ref, w2_ref, tail_ref):
    outs = []
    for c, x_ref in enumerate(x_refs):
        head = jnp.zeros((ncp, LANES), F32)
        tail = jnp.zeros((ncp, LANES), F32)
        for l in range(CMP_STRIDE):
            r = x_ref[pl.ds(l, ncp, stride=CMP_STRIDE), :]
            head = head + _dot(_bf(r + pos_ref[l, c]), w1_ref[l, c])
            tail = tail + _dot(_bf(r + pos_ref[CMP_STRIDE + l, c]), w1_ref[CMP_STRIDE + l, c])
        tail_ref[c, 0:ncp, :] = tail
        tail_ref[c, ncp:ncp + SUBLANES, :] = jnp.zeros((SUBLANES, LANES), F32)
        hid = jax.nn.gelu(head + tail_ref[c, pl.ds(1, ncp), :])
        outs.append(_dot(_bf(hid), w2_ref[c]))
    return outs


def _compress_kernel(xk_ref, xv_ref, w1_ref, pos_ref, w2_ref, o_ref, tail_ref, *, ncp):
    kc, vc = _compress_rows((xk_ref.at[0], xv_ref.at[0]), ncp, w1_ref, pos_ref, w2_ref, tail_ref)
    o_ref[0, :, 0:LANES] = kc
    o_ref[0, :, LANES:2 * LANES] = vc


def _compress_prompt(proj, w1bd, posrow, w2bd):
    bsz, t, _ = proj.shape
    ncp = t // CMP_STRIDE
    return pl.pallas_call(
        functools.partial(_compress_kernel, ncp=ncp),
        grid=(bsz,),
        in_specs=[pl.BlockSpec((1, t, LANES), lambda b: (b, 0, C_KVB // LANES)),
                  pl.BlockSpec((1, t, LANES), lambda b: (b, 0, C_KVB // LANES + 1)),
                  pl.BlockSpec((CMP_LEN, 2, LANES, LANES), lambda b: (0, 0, 0, 0)),
                  pl.BlockSpec((CMP_LEN, 2, 1, LANES), lambda b: (0, 0, 0, 0)),
                  pl.BlockSpec((2, LANES, LANES), lambda b: (0, 0, 0))],
        out_specs=pl.BlockSpec((1, ncp, 2 * LANES), lambda b: (b, 0, 0)),
        out_shape=jax.ShapeDtypeStruct((bsz, ncp, 2 * LANES), F32),
        scratch_shapes=[pltpu.VMEM((2, ncp + SUBLANES, LANES), F32)],
        compiler_params=_cparams(("parallel",)),
    )(proj, proj, w1bd, posrow, w2bd)


def _group_queries(q, g, nq):
    lane = lax.broadcasted_iota(jnp.int32, (nq, LANES), 1)
    in_g = (lane >= HEAD_DIM) if g else (lane < HEAD_DIM)
    return jnp.concatenate(
        [jnp.where(in_g, q[:, r * LANES:(r + 1) * LANES], 0.0) for r in range(R_NSA)], axis=0)


def _masked_softmax(s, mask):
    s = jnp.where(mask, s, NEG)
    m = jnp.max(s, axis=1, keepdims=True)
    p = jnp.where(mask, jnp.exp(s - m), 0.0)
    l = jnp.sum(p, axis=1, keepdims=True)
    return p / jnp.maximum(l, 1e-30)


def _select_blocks(imp, q_pos, n_blk):
    nq, nb = imp.shape
    blk = lax.broadcasted_iota(jnp.int32, (nq, nb), 1)
    blk_f = blk.astype(F32)
    cur = q_pos // SEL_BLOCK
    forced = (blk == 0) | (blk == cur) | (blk == cur - 1)
    valid = blk * SEL_BLOCK <= q_pos
    score = jnp.where(forced, FORCE_SCORE, jnp.where(valid, imp, -1.0))
    score = jnp.where(blk < n_blk, score, -jnp.inf)
    sel = jnp.zeros((nq, nb), F32)
    for _ in range(SEL_TOPK):
        m = jnp.max(score, axis=1, keepdims=True)
        first = jnp.min(jnp.where(score == m, blk_f, float(nb)), axis=1, keepdims=True)
        pick = blk_f == first
        sel = jnp.where(pick, 1.0, sel)
        score = jnp.where(pick, -jnp.inf, score)
    return sel


def _gate_combine(sig, o_c, o_s, o_w, nq):
    lane = lax.broadcasted_iota(jnp.int32, (nq, LANES), 1)
    slabs = []
    for r in range(R_NSA):
        per_g = []
        for g in range(G_NSA):
            h = g * R_NSA + r
            rows = slice(r * nq, (r + 1) * nq)
            acc = sig[:, h:h + 1] * o_c[g][rows]
            acc = acc + sig[:, H_NSA + h:H_NSA + h + 1] * o_s[g][rows]
            acc = acc + sig[:, 2 * H_NSA + h:2 * H_NSA + h + 1] * o_w[g][rows]
            per_g.append(acc)
        slabs.append(jnp.where(lane < HEAD_DIM, per_g[0], per_g[1]))
    return jnp.concatenate(slabs, axis=1)


def _nsa_prompt_kernel(q_ref, gate_ref, kcvc_ref, selkv_ref, w0_ref, w1_ref, w2_ref, w3_ref, w4_ref,
                       tt_ref, tca_ref, tcb_ref, tcf_ref, map_ref, o_ref,
                       m_ref, l_ref, acc_ref, *, ncp, n_blk):
    tq = TQ
    i = pl.program_id(1)
    q0 = i * tq
    rows = R_NSA * tq
    q = q_ref[0] * QK_SCALE
    lane = lax.broadcasted_iota(jnp.int32, (tq, LANES), 1)
    q_pos = q0 + lax.broadcasted_iota(jnp.int32, (tq, LANES), 0)
    nbp = map_ref.shape[1]
    q_pos_nb = q0 + lax.broadcasted_iota(jnp.int32, (tq, nbp), 0)
    win_refs = (w0_ref, w1_ref, w2_ref, w3_ref, w4_ref)
    n_ctile = ncp // LANES
    jn_a = i // 16
    o_c, o_s, o_w = [], [], []

    for g in range(G_NSA):
        qg = _group_queries(q, g, tq)
        qg_bf = _bf(qg)

        s = _dot3_t(qg, kcvc_ref[0, :, 0:LANES])
        n_idx = lax.broadcasted_iota(jnp.int32, (tq, ncp), 1)
        qp_c = q0 + lax.broadcasted_iota(jnp.int32, (tq, ncp), 0)
        mask_c = (n_idx * CMP_STRIDE + CMP_LEN - 1) <= qp_c
        bias_rows = []
        for r in range(R_NSA):
            h = g * R_NSA + r
            tiles = []
            for jn in range(n_ctile):
                t = jnp.where(jn == jn_a, tca_ref[0, h],
                              jnp.where(jn == jn_a - 1, tcb_ref[0, h], tcf_ref[0, h]))
                tiles.append(t)
            bias_rows.append(jnp.concatenate(tiles, axis=1) if n_ctile > 1 else tiles[0])
        bias_c = jnp.concatenate(bias_rows, axis=0)
        mask_c4 = jnp.concatenate([mask_c] * R_NSA, axis=0)
        p_c = _masked_softmax(s + bias_c, mask_c4)
        o_c.append(_dot(_bf(p_c), _bf(kcvc_ref[0, :, LANES:2 * LANES])))

        p_sum = p_c[0:tq]
        for r in range(1, R_NSA):
            p_sum = p_sum + p_c[r * tq:(r + 1) * tq]
        imp = _dot2_exact_rhs(p_sum, map_ref[...])
        sel_bf = _bf(_select_blocks(imp, q_pos_nb, n_blk))

        m_ref[...] = jnp.full((rows, LANES), NEG, F32)
        l_ref[...] = jnp.zeros((rows, LANES), F32)
        acc_ref[...] = jnp.zeros((rows, LANES), F32)
        blk_row = lax.broadcasted_iota(jnp.int32, (nbp, LANES), 0)
        blk_lane = lax.broadcasted_iota(jnp.int32, (nbp, LANES), 1) // SEL_BLOCK

        def sel_tile(j, causal):
            k0 = pl.multiple_of(j * LANES, LANES)
            kv = selkv_ref[0, pl.ds(k0, LANES), :]
            kt = _bf(kv[:, 0:LANES])
            vt = _bf(kv[:, LANES:2 * LANES])
            expand = jnp.where(blk_row == 2 * j + blk_lane, 1.0, 0.0).astype(BF16)
            mk = _dot(sel_bf, expand) > 0.5
            if causal:
                mk = jnp.logical_and(mk, (k0 + lane) <= q_pos)
            mk4 = jnp.concatenate([mk] * R_NSA, axis=0)
            d = jnp.minimum(i - j, N_TOK_TILES - 1)
            bias = jnp.concatenate([tt_ref[d, g * R_NSA + r] for r in range(R_NSA)], axis=0)
            sc = jnp.where(mk4, _dot_t(qg_bf, kt) + bias, NEG)
            m_old = m_ref[...]
            m_new = jnp.maximum(m_old, jnp.max(sc, axis=1, keepdims=True))
            alpha = jnp.exp(m_old - m_new)
            p = jnp.where(mk4, jnp.exp(sc - m_new), 0.0)
            l_ref[...] = alpha * l_ref[...] + jnp.sum(p, axis=1, keepdims=True)
            acc_ref[...] = alpha * acc_ref[...] + _dot(_bf(p), vt)
            m_ref[...] = m_new

        def body(j, c):
            sel_tile(j, False)
            return c

        lax.fori_loop(0, i, body, 0)
        sel_tile(i, True)
        o_s.append(acc_ref[...] / l_ref[...])

        s_tiles, v_tiles, m_tiles = [], [], []
        for dlt in range(WINDOW // LANES + 1):
            kv = win_refs[dlt][0]
            kt = _bf(kv[:, 0:LANES])
            v_tiles.append(_bf(kv[:, LANES:2 * LANES]))
            dist = q_pos - ((i - dlt) * LANES + lane)
            mk = (dist >= 0) & (dist < WINDOW) & (i - dlt >= 0)
            bias = jnp.concatenate([tt_ref[dlt, g * R_NSA + r] for r in range(R_NSA)], axis=0)
            s_tiles.append(_dot_t(qg_bf, kt) + bias)
            m_tiles.append(jnp.concatenate([mk] * R_NSA, axis=0))
        p_w = _masked_softmax(jnp.concatenate(s_tiles, axis=1), jnp.concatenate(m_tiles, axis=1))
        o_w.append(_dot(_bf(p_w), jnp.concatenate(v_tiles, axis=0)))

    sig = jax.nn.sigmoid(gate_ref[0])
    o_ref[0] = _gate_combine(sig, o_c, o_s, o_w, tq)


def _nsa_prompt(proj, kcvc, tt, tc, sel_map):
    bsz, t, _ = proj.shape
    tq = TQ
    ncp = t // CMP_STRIDE
    n_blk = t // SEL_BLOCK
    nbp = sel_map.shape[1]
    rows = R_NSA * tq
    nwin = WINDOW // LANES + 1
    win_specs = [
        pl.BlockSpec((1, tq, 2 * LANES),
                     functools.partial(lambda b, i, d: (b, jnp.maximum(i - d, 0), (C_KVB + 4 * LANES) // (2 * LANES)), d=d))
        for d in range(nwin)]
    far = N_CMP_TILES - 1
    return pl.pallas_call(
        functools.partial(_nsa_prompt_kernel, ncp=ncp, n_blk=n_blk),
        grid=(bsz, t // tq),
        in_specs=[pl.BlockSpec((1, tq, W_NSA), lambda b, i: (b, i, C_QB // W_NSA)),
                  pl.BlockSpec((1, tq, LANES), lambda b, i: (b, i, C_GATE // LANES)),
                  pl.BlockSpec((1, ncp, 2 * LANES), lambda b, i: (b, 0, 0)),
                  pl.BlockSpec((1, t, 2 * LANES), lambda b, i: (b, 0, (C_KVB + 2 * LANES) // (2 * LANES)))]
                 + win_specs +
                 [pl.BlockSpec((N_TOK_TILES, H_NSA, tq, LANES), lambda b, i: (0, 0, 0, 0)),
                  pl.BlockSpec((1, H_NSA, tq, LANES), lambda b, i: (i % 16, 0, 0, 0)),
                  pl.BlockSpec((1, H_NSA, tq, LANES), lambda b, i: (jnp.minimum(i % 16 + 16, far), 0, 0, 0)),
                  pl.BlockSpec((1, H_NSA, tq, LANES), lambda b, i: (far, 0, 0, 0)),
                  pl.BlockSpec((ncp, nbp), lambda b, i: (0, 0))],
        out_specs=pl.BlockSpec((1, tq, W_NSA), lambda b, i: (b, i, 0)),
        out_shape=jax.ShapeDtypeStruct((bsz, t, W_NSA), F32),
        scratch_shapes=[pltpu.VMEM((rows, LANES), F32)] * 3,
        compiler_params=_cparams(("parallel", "arbitrary")),
    )(proj, proj, kcvc, proj, *([proj] * nwin), tt, tc, tc, tc, sel_map)


def _nsa_sample_kernel(pt_ref, p_ref, win_ref, cache_ref, w1_ref, pos_ref, w2_ref, map_ref,
                       bsel_ref, bcmp_ref, bwin_ref, o_ref, buf, sem, tail_ref, wbuf,
                       *, layer_base, n_pages, nq, n_blk):
    b = pl.program_id(0)
    past = n_pages * PAGE_SIZE
    tk = past + LANES
    ncp = past // CMP_STRIDE
    nbp = map_ref.shape[1]
    rows = R_NSA * nq
    wlen = win_ref.shape[1]
    wpad = wbuf.shape[0]

    n_slab = buf.shape[0]

    def page_copy(j, s):
        return pltpu.make_async_copy(
            cache_ref.at[layer_base + pt_ref[b, j], :, pl.ds(s * LANES, LANES)],
            buf.at[s, pl.ds(pl.multiple_of(j * PAGE_SIZE, PAGE_SIZE), PAGE_SIZE)], sem)

    def start(j, c):
        for s in range(n_slab):
            page_copy(j, s).start()
        return c

    def wait(j, c):
        for s in range(n_slab):
            page_copy(j, s).wait()
        return c

    lax.fori_loop(0, n_pages, start, 0)

    new_nsa = p_ref[0, :, C_KVB - C_QB:C_KVB - C_QB + 4 * LANES]
    for s in range(n_slab):
        buf[s, past:past + nq, :] = new_nsa[:, s * LANES:(s + 1) * LANES]
        buf[s, past + nq:tk, :] = jnp.zeros((LANES - nq, LANES), F32)
    wbuf[0:wlen, :] = win_ref[0]
    wbuf[wlen:wlen + nq, :] = p_ref[0, :, C_KVB - C_QB + 4 * LANES:C_KVB - C_QB + 6 * LANES]
    wbuf[wlen + nq:wpad, :] = jnp.zeros((wpad - wlen - nq, 2 * LANES), F32)

    q = p_ref[0, :, 0:W_NSA] * QK_SCALE
    sig = jax.nn.sigmoid(p_ref[0, :, C_GATE - C_QB:C_GATE - C_QB + LANES])
    q_pos_nb = past + lax.broadcasted_iota(jnp.int32, (nq, nbp), 0)

    lax.fori_loop(0, n_pages, wait, 0)

    kc, vc = _compress_rows((buf.at[0], buf.at[1]), ncp, w1_ref, pos_ref, w2_ref, tail_ref)
    n_idx = lax.broadcasted_iota(jnp.int32, (rows, ncp), 1)
    mask_c = n_idx < ncp - 1
    tok = lax.broadcasted_iota(jnp.int32, (rows, tk), 1)
    qi_tok = lax.broadcasted_iota(jnp.int32, (rows, tk), 0) % nq
    causal = tok <= past + qi_tok
    kk = lax.broadcasted_iota(jnp.int32, (rows, wpad), 1)
    dist_w = wlen + (lax.broadcasted_iota(jnp.int32, (rows, wpad), 0) % nq) - kk
    mask_w = (dist_w >= 0) & (dist_w < WINDOW) & (kk < wlen + nq)
    blk_row = lax.broadcasted_iota(jnp.int32, (nbp, tk), 0)
    blk_tok = lax.broadcasted_iota(jnp.int32, (nbp, tk), 1) // SEL_BLOCK
    expand = jnp.where(blk_row == blk_tok, 1.0, 0.0).astype(BF16)

    o_c, o_s, o_w = [], [], []
    for g in range(G_NSA):
        qg = _group_queries(q, g, nq)
        qg_bf = _bf(qg)
        brow = slice(g * rows, (g + 1) * rows)
        s = _dot3_t(qg, kc) + bcmp_ref[brow, :]
        p_c = _masked_softmax(s, mask_c)
        o_c.append(_dot(_bf(p_c), _bf(vc)))
        p_sum = p_c[0:nq]
        for r in range(1, R_NSA):
            p_sum = p_sum + p_c[r * nq:(r + 1) * nq]
        imp = _dot2_exact_rhs(p_sum, map_ref[...])
        sel = _select_blocks(imp, q_pos_nb, n_blk)
        mk = _dot(_bf(jnp.concatenate([sel] * R_NSA, axis=0)), expand) > 0.5
        mk = jnp.logical_and(mk, causal)
        s = _dot_t(qg_bf, _bf(buf[2])) + bsel_ref[brow, :]
        p_s = _masked_softmax(s, mk)
        o_s.append(_dot(_bf(p_s), _bf(buf[3])))
        s = _dot_t(qg_bf, _bf(wbuf[:, 0:LANES])) + bwin_ref[brow, :]
        p_w = _masked_softmax(s, mask_w)
        o_w.append(_dot(_bf(p_w), _bf(wbuf[:, LANES:2 * LANES])))

    o_ref[0] = _gate_combine(sig, o_c, o_s, o_w, nq)


def _nsa_sample(proj_s, win_cache, cache_pages, page_table, w1bd, posrow, w2bd, sel_map,
                bsel, bcmp, bwin, layer, nq):
    dbsz, n_pages = page_table.shape
    n_pool = cache_pages.shape[0] // (win_cache.shape[0] // dbsz)
    past = n_pages * PAGE_SIZE
    tk = past + LANES
    ncp = past // CMP_STRIDE
    wlen = win_cache.shape[1]
    wpad = wlen + LANES
    n_blk = -(-(past + nq) // SEL_BLOCK)
    kern = functools.partial(_nsa_sample_kernel, layer_base=layer * n_pool, n_pages=n_pages,
                             nq=nq, n_blk=n_blk)
    const2 = lambda b, pt: (0, 0)
    grid_spec = pltpu.PrefetchScalarGridSpec(
        num_scalar_prefetch=1,
        grid=(dbsz,),
        in_specs=[pl.BlockSpec((1, nq, N_PROJ - C_QB), lambda b, pt: (0, b, 1)),
                  pl.BlockSpec((1, wlen, 2 * LANES), lambda b, pt: (layer * dbsz + b, 0, 0)),
                  pl.BlockSpec(memory_space=pl.ANY),
                  pl.BlockSpec((CMP_LEN, 2, LANES, LANES), lambda b, pt: (0, 0, 0, 0)),
                  pl.BlockSpec((CMP_LEN, 2, 1, LANES), lambda b, pt: (0, 0, 0, 0)),
                  pl.BlockSpec((2, LANES, LANES), lambda b, pt: (0, 0, 0)),
                  pl.BlockSpec(sel_map.shape, const2),
                  pl.BlockSpec(bsel.shape, const2),
                  pl.BlockSpec(bcmp.shape, const2),
                  pl.BlockSpec(bwin.shape, const2)],
        out_specs=pl.BlockSpec((1, nq, W_NSA), lambda b, pt: (0, b, 0)),
        scratch_shapes=[pltpu.VMEM((4, tk, LANES), F32),
                        pltpu.SemaphoreType.DMA(()),
                        pltpu.VMEM((2, ncp + SUBLANES, LANES), F32),
                        pltpu.VMEM((wpad, 2 * LANES), F32)],
    )
    return pl.pallas_call(
        kern, grid_spec=grid_spec,
        out_shape=jax.ShapeDtypeStruct((1, dbsz * nq, W_NSA), F32),
        compiler_params=_cparams(("arbitrary",)),
    )(page_table, proj_s, win_cache, cache_pages, w1bd, posrow, w2bd, sel_map, bsel, bcmp, bwin)


def _out_kernel(y_ref, oa_ref, ob_ref, za_ref, zb_ref, gate_ref, norm_ref, w_ref, g_ref, b_ref, o_ref,
                *, alpha):
    def gated_norm(o, z, nrm):
        ms = jnp.mean(o * o, axis=-1, keepdims=True)
        return o * lax.rsqrt(ms + RMS_EPS) * nrm * (z * jax.nn.sigmoid(z))

    ua = gated_norm(oa_ref[0], za_ref[0], norm_ref[:, 0:W_SB])
    ub = gated_norm(ob_ref[0], zb_ref[0], norm_ref[:, W_SB:D_MODEL])
    u = _bf(jnp.concatenate([ua, ub], axis=1))
    mixed = _dot(u, w_ref[...])
    x = alpha * y_ref[0] + gate_ref[0] * mixed
    mu = jnp.mean(x, axis=-1, keepdims=True)
    xc = x - mu
    var = jnp.mean(xc * xc, axis=-1, keepdims=True)
    o_ref[0] = xc * lax.rsqrt(var + LN_EPS) * g_ref[...] + b_ref[...]


def _out_stage(y, o_a, o_b, proj, gate, norm_p, w_out_bf, ln_g, ln_b, alpha):
    bsz, t, d = y.shape
    tm = min(256, t)
    if gate.shape[1] == 1:
        gate_spec = pl.BlockSpec((1, 1, d), lambda b, i: (b, 0, 0))
    else:
        gate_spec = pl.BlockSpec((1, tm, d), lambda b, i: (b, i, 0))
    row = lambda w: pl.BlockSpec((1, w), lambda b, i: (0, 0))
    return pl.pallas_call(
        functools.partial(_out_kernel, alpha=alpha),
        grid=(bsz, t // tm),
        in_specs=[pl.BlockSpec((1, tm, d), lambda b, i: (b, i, 0)),
                  pl.BlockSpec((1, tm, W_SB), lambda b, i: (b, i, 0)),
                  pl.BlockSpec((1, tm, W_NSA), lambda b, i: (b, i, 0)),
                  pl.BlockSpec((1, tm, W_SB), lambda b, i: (b, i, C_ZA // W_SB)),
                  pl.BlockSpec((1, tm, W_NSA), lambda b, i: (b, i, C_ZB // W_NSA)),
                  gate_spec, row(d),
                  pl.BlockSpec((d, d), lambda b, i: (0, 0)),
                  row(d), row(d)],
        out_specs=pl.BlockSpec((1, tm, d), lambda b, i: (b, i, 0)),
        out_shape=jax.ShapeDtypeStruct((bsz, t, d), F32),
        compiler_params=_cparams(("parallel", "parallel")),
    )(y, o_a, o_b, proj, proj, gate, norm_p.reshape(1, d), w_out_bf, ln_g.reshape(1, d), ln_b.reshape(1, d))


def _rel_bucket(dist):
    n = jnp.maximum(dist, 0)
    nf = jnp.maximum(n, 1).astype(F32)
    large = REL_MAX_EXACT + (jnp.log(nf / REL_MAX_EXACT) / math.log(REL_MAX_DIST / REL_MAX_EXACT)
                             * (N_BUCKETS - REL_MAX_EXACT)).astype(jnp.int32)
    large = jnp.minimum(large, N_BUCKETS - 1)
    return jnp.where(n < REL_MAX_EXACT, n, large)


def _nsa_perm(x, axis):
    shp = x.shape
    x = x.reshape(shp[:axis] + (G_NSA, R_NSA, HEAD_DIM) + shp[axis + 1:])
    x = jnp.swapaxes(x, axis, axis + 1)
    return x.reshape(shp)


def _bias_tables(rel_bias, nq, past, wlen):
    bv = rel_bias[_rel_bucket(jnp.arange(BIAS_DMAX))].T
    look = lambda dist: bv[:, np.clip(dist, 0, BIAS_DMAX - 1)]
    qi = np.arange(TQ)[:, None]
    ki = np.arange(LANES)[None, :]
    tt = jnp.stack([look(LANES * d + qi - ki) for d in range(N_TOK_TILES)])
    tc = jnp.stack([look(LANES * o - (CMP_LEN - 1) + qi - CMP_STRIDE * ki)
                    for o in range(N_CMP_TILES)])
    sq = np.arange(nq)[:, None]
    rows = lambda x: x.reshape(H_NSA * nq, x.shape[-1])
    bsel = rows(look(past + sq - np.arange(past + LANES)[None, :]))
    bcmp = rows(look(past + sq - CMP_STRIDE * np.arange(past // CMP_STRIDE)[None, :] - (CMP_LEN - 1)))
    bwin = rows(look(wlen + sq - np.arange(wlen + LANES)[None, :]))
    return tt, tc, bsel, bcmp, bwin


def _selection_map(ncp, n_c, n_blk):
    nbp = -(-n_blk // LANES) * LANES
    cs = np.arange(ncp)[:, None] * CMP_STRIDE
    ss = np.arange(nbp)[None, :] * SEL_BLOCK
    ov = np.minimum(cs + CMP_LEN, ss + SEL_BLOCK) - np.maximum(cs, ss)
    m = np.clip(ov, 0, None).astype(np.float32) / CMP_LEN
    m = m * (np.arange(ncp)[:, None] < n_c) * (np.arange(nbp)[None, :] < n_blk)
    return jnp.asarray(m, dtype=BF16)


def _block_diag2(w):
    z = jnp.zeros_like(w)
    return jnp.concatenate([jnp.concatenate([w, z], axis=-1), jnp.concatenate([z, w], axis=-1)], axis=-2)


def kernel(x_prompt, x_sample, cache_sb_kv, cache_nsa_kv, cache_win_kv, page_table, c_prompt, c_sample,
           ln_in_g, ln_in_b, w_ada, b_ada, w_in, w_cmp1, w_cmp2, pos_cmp, norm_grp, w_out, ln_g, ln_b,
           rel_bias):
    bsz, seq, d = x_prompt.shape
    dbsz, nq, _ = x_sample.shape
    depth = w_ada.shape[0]
    n_pool = cache_sb_kv.shape[1]
    n_pages = page_table.shape[1]
    past = n_pages * PAGE_SIZE
    wlen = cache_win_kv.shape[2]
    alpha = (2 * depth) ** 0.25
    assert d == D_MODEL and seq % (16 * TQ) == 0 and past % (16 * TQ) == 0 and wlen == WINDOW
    assert seq // SEL_BLOCK >= SEL_TOPK and nq == SUBLANES

    w_qb = _nsa_perm(w_in[:, :, 2048:2560], 2)
    w_zb = _nsa_perm(w_in[:, :, 3352:3864], 2)
    pad = lambda n: jnp.zeros((depth, d, n), F32)
    w_in_p = _bf(jnp.concatenate(
        [w_in[:, :, 0:2048], w_qb, w_in[:, :, 2560:3328], w_in[:, :, 3328:3352], pad(C_ZB - C_GATE - 3 * H_NSA),
         w_zb], axis=2))
    norm_p = jnp.concatenate([norm_grp[:, :W_SB], _nsa_perm(norm_grp[:, W_SB:], 1)], axis=1)
    w_out_p = _bf(jnp.concatenate([w_out[:, :W_SB], _nsa_perm(w_out[:, W_SB:], 1)], axis=1))
    w1bd = _bf(jnp.swapaxes(_block_diag2(w_cmp1), 1, 2))
    w2bd = _bf(_block_diag2(w_cmp2))
    posrow = jnp.swapaxes(jnp.concatenate([pos_cmp, pos_cmp], axis=-1), 1, 2)[:, :, :, None, :]
    tt, tc, bsel, bcmp, bwin = _bias_tables(rel_bias, nq, past, wlen)
    map_p = _selection_map(seq // CMP_STRIDE, (seq - CMP_LEN) // CMP_STRIDE + 1, seq // SEL_BLOCK)
    n_blk_s = -(-(past + nq) // SEL_BLOCK)
    map_s = _selection_map(past // CMP_STRIDE, (past + nq - CMP_LEN) // CMP_STRIDE + 1, n_blk_s)

    n_c = bsz + dbsz
    c_rows = -(-n_c // SUBLANES) * SUBLANES
    c_all = jnp.concatenate([c_prompt, c_sample, jnp.zeros((c_rows - n_c, d), F32)], axis=0)
    mod = _ada_mod(c_all, w_ada, b_ada)

    y_p = _layer_norm(x_prompt.reshape(bsz * seq, d), ln_in_g, ln_in_b).reshape(bsz, seq, d)
    y_s = _layer_norm(x_sample.reshape(dbsz * nq, d), ln_in_g, ln_in_b).reshape(1, dbsz * nq, d)

    sb_pages = cache_sb_kv.reshape(depth * n_pool, PAGE_SIZE, 2 * W_SB)
    nsa_pages = cache_nsa_kv.reshape(depth * n_pool, PAGE_SIZE, 4 * KV_NSA)
    win_cache = cache_win_kv.reshape(depth * dbsz, wlen, 2 * KV_NSA)

    p_sb, p_nsa, p_win, s_sb, s_nsa, s_win = [], [], [], [], [], []
    for l in range(depth):
        mp = mod[l, :bsz].reshape(bsz, 1, 3 * d)
        proj = _in_proj(y_p, mp[:, :, 0:d], mp[:, :, d:2 * d], w_in_p[l])
        o_a = _sb_prompt(proj)
        kcvc = _compress_prompt(proj, w1bd[l], posrow[l], w2bd[l])
        o_b = _nsa_prompt(proj, kcvc, tt, tc, map_p)
        p_sb.append(proj[:, :, C_KA:C_ZA].reshape(bsz, seq, 2, H_SB, HEAD_DIM))
        p_nsa.append(proj[:, :, C_KVB:C_KVB + 4 * KV_NSA].reshape(bsz, seq, 4, G_NSA, HEAD_DIM))
        p_win.append(proj[:, seq - min(WINDOW, seq):, C_KVB + 4 * KV_NSA:C_KVB + 6 * KV_NSA]
                     .reshape(bsz, min(WINDOW, seq), 2, G_NSA, HEAD_DIM))
        y_p = _out_stage(y_p, o_a, o_b, proj, mp[:, :, 2 * d:3 * d], norm_p[l], w_out_p[l], ln_g[l], ln_b[l],
                         alpha)
        ms = jnp.repeat(mod[l, bsz:bsz + dbsz], nq, axis=0).reshape(1, dbsz * nq, 3 * d)
        proj_s = _in_proj(y_s, ms[:, :, 0:d], ms[:, :, d:2 * d], w_in_p[l])
        o_a = _sb_sample(proj_s, sb_pages, page_table, l * n_pool, nq)
        o_b = _nsa_sample(proj_s, win_cache, nsa_pages, page_table, w1bd[l], posrow[l], w2bd[l], map_s,
                          bsel, bcmp, bwin, l, nq)
        ps = proj_s.reshape(dbsz, nq, N_PROJ)
        s_sb.append(ps[:, :, C_KA:C_ZA].reshape(dbsz, nq, 2, H_SB, HEAD_DIM))
        s_nsa.append(ps[:, :, C_KVB:C_KVB + 4 * KV_NSA].reshape(dbsz, nq, 4, G_NSA, HEAD_DIM))
        new_win = ps[:, :, C_KVB + 4 * KV_NSA:C_KVB + 6 * KV_NSA].reshape(dbsz, nq, 2, G_NSA, HEAD_DIM)
        win_all = jnp.concatenate([cache_win_kv[l], new_win], axis=1)
        s_win.append(win_all[:, win_all.shape[1] - min(WINDOW, past + nq):])
        y_s = _out_stage(y_s, o_a, o_b, proj_s, ms[:, :, 2 * d:3 * d], norm_p[l], w_out_p[l], ln_g[l],
                         ln_b[l], alpha)

    return (y_p, y_s.reshape(dbsz, nq, d), jnp.stack(p_sb), jnp.stack(p_nsa), jnp.stack(p_win),
            jnp.stack(s_sb), jnp.stack(s_nsa), jnp.stack(s_win))
```

```python
import functools
import math

import numpy as np
import jax
import jax.numpy as jnp
from jax import lax
from jax.experimental import pallas as pl
from jax.experimental.pallas import tpu as pltpu

F32 = jnp.float32
BF16 = jnp.bfloat16

D_MODEL = 1024
HEAD_DIM = 64
W_SB = D_MODEL // 2
W_NSA = D_MODEL - W_SB
H_SB = W_SB // HEAD_DIM
H_NSA = W_NSA // HEAD_DIM
G_NSA = 2
R_NSA = H_NSA // G_NSA
KV_NSA = G_NSA * HEAD_DIM
CMP_LEN = 32
CMP_STRIDE = 16
SEL_BLOCK = 64
SEL_TOPK = 16
WINDOW = 512
PAGE_SIZE = 128
N_BUCKETS = 32
REL_MAX_EXACT = N_BUCKETS // 2
REL_MAX_DIST = 1024
LN_EPS = 1e-5
RMS_EPS = 1e-6
NEG = -1e30
MASK_VALUE = -1e30
FORCE_SCORE = 1e4
QK_SCALE = HEAD_DIM ** -0.5
LOG2E = math.log2(math.e)

LANES = 128
SUBLANES = 8
VMEM_LIMIT = 56 * 1024 * 1024

C_QA, C_KA, C_VA, C_ZA, C_QB, C_KVB = 0, 512, 1024, 1536, 2048, 2560
C_GATE, C_ZB, N_PROJ = 3328, 3584, 4096
SB_DEAD = -104.0
BIAS_DMAX = 1024
N_TOK_TILES = 9
N_CMP_TILES = 24
TQ = 128


def _bf(x):
    return x.astype(BF16)


def _dot(a, b):
    return jnp.dot(a, b, preferred_element_type=F32)


def _dot_t(a, b):
    return lax.dot_general(a, b, (((1,), (1,)), ((), ())), preferred_element_type=F32)


def _split(x):
    hi = _bf(x)
    lo = _bf(x - hi.astype(F32))
    return hi, lo


def _dot3_t(a, b):
    ah, al = _split(a)
    bh, bl = _split(b)
    return _dot_t(ah, bh) + _dot_t(ah, bl) + _dot_t(al, bh)


def _dot3(a, b):
    ah, al = _split(a)
    bh, bl = _split(b)
    return _dot(ah, bh) + _dot(ah, bl) + _dot(al, bh)


def _dot2_exact_rhs(a, b_bf):
    ah, al = _split(a)
    return _dot(ah, b_bf) + _dot(al, b_bf)


def _softplus(z):
    return jnp.maximum(z, 0.0) + jnp.log1p(jnp.exp(-jnp.abs(z)))


def _cparams(sem):
    return pltpu.CompilerParams(dimension_semantics=sem, vmem_limit_bytes=VMEM_LIMIT)


def _ln_kernel(x_ref, g_ref, b_ref, o_ref):
    x = x_ref[...]
    mu = jnp.mean(x, axis=-1, keepdims=True)
    xc = x - mu
    var = jnp.mean(xc * xc, axis=-1, keepdims=True)
    o_ref[...] = xc * lax.rsqrt(var + LN_EPS) * g_ref[...] + b_ref[...]


def _layer_norm(x2d, g, b):
    rows, d = x2d.shape
    tm = min(512, rows)
    return pl.pallas_call(
        _ln_kernel,
        grid=(rows // tm,),
        in_specs=[pl.BlockSpec((tm, d), lambda i: (i, 0)),
                  pl.BlockSpec((1, d), lambda i: (0, 0)),
                  pl.BlockSpec((1, d), lambda i: (0, 0))],
        out_specs=pl.BlockSpec((tm, d), lambda i: (i, 0)),
        out_shape=jax.ShapeDtypeStruct((rows, d), F32),
        compiler_params=_cparams(("parallel",)),
    )(x2d, g.reshape(1, d), b.reshape(1, d))


def _mod_kernel(c_ref, w_ref, b_ref, o_ref):
    c = c_ref[...]
    s = c * jax.nn.sigmoid(c)
    o_ref[0] = _dot3(s, w_ref[0]) + b_ref[0]


def _ada_mod(c_all, w_ada, b_ada):
    depth, d, n3 = w_ada.shape
    rows = c_all.shape[0]
    tn = 1024
    return pl.pallas_call(
        _mod_kernel,
        grid=(depth, n3 // tn),
        in_specs=[pl.BlockSpec((rows, d), lambda l, j: (0, 0)),
                  pl.BlockSpec((1, d, tn), lambda l, j: (l, 0, j)),
                  pl.BlockSpec((1, 1, tn), lambda l, j: (l, 0, j))],
        out_specs=pl.BlockSpec((1, rows, tn), lambda l, j: (l, 0, j)),
        out_shape=jax.ShapeDtypeStruct((depth, rows, n3), F32),
        compiler_params=_cparams(("parallel", "parallel")),
    )(c_all, w_ada, b_ada.reshape(depth, 1, n3))


def _inproj_kernel(y_ref, sh_ref, sc_ref, w_ref, o_ref):
    h = y_ref[0] * (1.0 + sc_ref[0]) + sh_ref[0]
    hb = _bf(h)
    chunk = 512
    for c0 in range(0, N_PROJ, chunk):
        o_ref[0, :, c0:c0 + chunk] = _dot(hb, w_ref[:, c0:c0 + chunk])


def _in_proj(y, shift, scale, w_bf):
    bsz, t, d = y.shape
    tm = min(256, t)
    tmod = shift.shape[1]
    if tmod == 1:
        mod_spec = pl.BlockSpec((1, 1, d), lambda b, i: (b, 0, 0))
    else:
        mod_spec = pl.BlockSpec((1, tm, d), lambda b, i: (b, i, 0))
    return pl.pallas_call(
        _inproj_kernel,
        grid=(bsz, t // tm),
        in_specs=[pl.BlockSpec((1, tm, d), lambda b, i: (b, i, 0)), mod_spec, mod_spec,
                  pl.BlockSpec((d, N_PROJ), lambda b, i: (0, 0))],
        out_specs=pl.BlockSpec((1, tm, N_PROJ), lambda b, i: (b, i, 0)),
        out_shape=jax.ShapeDtypeStruct((bsz, t, N_PROJ), F32),
        compiler_params=_cparams(("parallel", "parallel")),
    )(y, shift, scale, w_bf)


def _sb_tile(qh, kt, vt, mask, carry, tri):
    z = _dot_t(qh, kt)
    lk = jnp.where(mask, -_softplus(z), 0.0)
    hi, lo = _split(lk)
    after = _dot(hi, tri) + _dot(lo, tri) + carry
    a = jnp.where(mask, jnp.exp(z + lk + after), 0.0)
    contrib = _dot(_bf(a), vt)
    carry = carry + jnp.sum(lk, axis=1, keepdims=True)
    return contrib, carry


def _tri_newer():
    r = lax.broadcasted_iota(jnp.int32, (LANES, LANES), 0)
    c = lax.broadcasted_iota(jnp.int32, (LANES, LANES), 1)
    return jnp.where(r > c, 1.0, 0.0).astype(BF16)


def _sb_prompt_kernel(q_ref, k_ref, v_ref, o_ref, *, tq):
    i = pl.program_id(2)
    q0 = i * tq
    q = q_ref[0] * QK_SCALE
    lane = lax.broadcasted_iota(jnp.int32, (tq, LANES), 1)
    row_pos = q0 + lax.broadcasted_iota(jnp.int32, (tq, LANES), 0)
    tri = _tri_newer()
    qh = _bf(jnp.concatenate([jnp.where(lane < HEAD_DIM, q, 0.0), jnp.where(lane >= HEAD_DIM, q, 0.0)], axis=0))
    lane2 = jnp.concatenate([lane, lane], axis=0)
    row_pos2 = jnp.concatenate([row_pos, row_pos], axis=0)

    def cond(c):
        j, alive, _, _ = c
        return jnp.logical_and(j >= 0, alive)

    def body(c):
        j, _, carry, acc = c
        k0 = pl.multiple_of(j * LANES, LANES)
        kt = _bf(k_ref[0, pl.ds(k0, LANES), :])
        vt = _bf(v_ref[0, pl.ds(k0, LANES), :])
        mask = (k0 + lane2) < row_pos2
        contrib, carry = _sb_tile(qh, kt, vt, mask, carry, tri)
        return j - 1, jnp.max(carry) > SB_DEAD, carry, acc + contrib

    j0 = (q0 + tq - 1) // LANES
    zero = jnp.zeros((2 * tq, LANES), F32)
    _, _, _, acc = lax.while_loop(cond, body, (j0, True, zero, zero))
    o_ref[0] = jnp.where(lane < HEAD_DIM, acc[0:tq], acc[tq:2 * tq])


def _sb_prompt(proj, tq=256):
    bsz, t, _ = proj.shape
    tq = min(tq, t)
    npair = W_SB // LANES
    return pl.pallas_call(
        functools.partial(_sb_prompt_kernel, tq=tq),
        grid=(bsz, npair, t // tq),
        in_specs=[pl.BlockSpec((1, tq, LANES), lambda b, p, i: (b, i, C_QA // LANES + p)),
                  pl.BlockSpec((1, t, LANES), lambda b, p, i: (b, 0, C_KA // LANES + p)),
                  pl.BlockSpec((1, t, LANES), lambda b, p, i: (b, 0, C_VA // LANES + p))],
        out_specs=pl.BlockSpec((1, tq, LANES), lambda b, p, i: (b, i, p)),
        out_shape=jax.ShapeDtypeStruct((bsz, t, W_SB), F32),
        compiler_params=_cparams(("parallel", "parallel", "arbitrary")),
    )(proj, proj, proj)


def _sb_sample_kernel(pt_ref, p_ref, last_ref, cache_ref, o_ref, buf, sem, carry_ref, acc_ref,
                      *, layer_base, n_pages, nq):
    b = pl.program_id(0)
    rows = H_SB * nq
    q = p_ref[0, :, C_QA:C_QA + W_SB] * QK_SCALE
    lane_c = lax.broadcasted_iota(jnp.int32, (rows, W_SB), 1)
    row_c = lax.broadcasted_iota(jnp.int32, (rows, W_SB), 0)
    own = (lane_c // HEAD_DIM) == (row_c // nq)
    qbd = _bf(jnp.where(own, jnp.concatenate([q] * H_SB, axis=0), 0.0))
    tri = _tri_newer()
    key = lax.broadcasted_iota(jnp.int32, (rows, LANES), 1)
    qi = lax.broadcasted_iota(jnp.int32, (rows, LANES), 0) % nq

    pad = jnp.zeros((LANES - nq, W_SB), F32)
    k_new = _bf(jnp.concatenate([p_ref[0, :, C_KA:C_KA + W_SB], pad], axis=0))
    v_new = _bf(jnp.concatenate([p_ref[0, :, C_VA:C_VA + W_SB], pad], axis=0))
    contrib, carry = _sb_tile(qbd, k_new, v_new, key < qi, jnp.zeros((rows, LANES), F32), tri)
    carry_ref[...] = carry
    acc_ref[...] = contrib

    def page_tile(kv_ref):
        kt = _bf(kv_ref[:, 0:W_SB])
        vt = _bf(kv_ref[:, W_SB:2 * W_SB])
        contrib, carry = _sb_tile(qbd, kt, vt, key >= 0, carry_ref[...], tri)
        carry_ref[...] = carry
        acc_ref[...] += contrib

    @pl.when(jnp.max(carry_ref[...]) > SB_DEAD)
    def _():
        page_tile(last_ref.at[0])

    def cond(c):
        j, alive = c
        return jnp.logical_and(j >= 0, alive)

    def body(c):
        j, _ = c
        page = layer_base + pt_ref[b, j]
        cp = pltpu.make_async_copy(cache_ref.at[page], buf, sem)
        cp.start()
        cp.wait()
        page_tile(buf)
        return j - 1, jnp.max(carry_ref[...]) > SB_DEAD

    lax.while_loop(cond, body, (n_pages - 2, jnp.max(carry_ref[...]) > SB_DEAD))

    acc = jnp.where(own, acc_ref[...], 0.0)
    o = acc[0:nq]
    for h in range(1, H_SB):
        o = o + acc[h * nq:(h + 1) * nq]
    o_ref[0] = o


def _sb_sample(proj_s, cache_pages, page_table, layer_base, nq):
    dbsz, n_pages = page_table.shape
    rows = H_SB * nq
    kern = functools.partial(_sb_sample_kernel, layer_base=layer_base, n_pages=n_pages, nq=nq)
    grid_spec = pltpu.PrefetchScalarGridSpec(
        num_scalar_prefetch=1,
        grid=(dbsz,),
        in_specs=[pl.BlockSpec((1, nq, C_QB), lambda b, pt: (0, b, 0)),
                  pl.BlockSpec((1, PAGE_SIZE, 2 * W_SB),
                               lambda b, pt: (layer_base + pt[b, n_pages - 1], 0, 0)),
                  pl.BlockSpec(memory_space=pl.ANY)],
        out_specs=pl.BlockSpec((1, nq, W_SB), lambda b, pt: (0, b, 0)),
        scratch_shapes=[pltpu.VMEM((PAGE_SIZE, 2 * W_SB), F32),
                        pltpu.SemaphoreType.DMA(()),
                        pltpu.VMEM((rows, LANES), F32),
                        pltpu.VMEM((rows, W_SB), F32)],
    )
    return pl.pallas_call(
        kern, grid_spec=grid_spec,
        out_shape=jax.ShapeDtypeStruct((1, dbsz * nq, W_SB), F32),
        compiler_params=_cparams(("arbitrary",)),
    )(page_table, proj_s, cache_pages, cache_pages)


def _compress_rows(x_refs, ncp, w1_ref, pos_ref, w2_ref, tail_ref):
    outs = []
    for c, x_ref in enumerate(x_refs):
        head = jnp.zeros((ncp, LANES), F32)
        tail = jnp.zeros((ncp, LANES), F32)
        for l in range(CMP_STRIDE):
            r = x_ref[pl.ds(l, ncp, stride=CMP_STRIDE), :]
            head = head + _dot(_bf(r + pos_ref[l, c]), w1_ref[l, c])
            tail = tail + _dot(_bf(r + pos_ref[CMP_STRIDE + l, c]), w1_ref[CMP_STRIDE + l, c])
        tail_ref[c, 0:ncp, :] = tail
        tail_ref[c, ncp:ncp + SUBLANES, :] = jnp.zeros((SUBLANES, LANES), F32)
        hid = jax.nn.gelu(head + tail_ref[c, pl.ds(1, ncp), :])
        outs.append(_dot(_bf(hid), w2_ref[c]))
    return outs


def _compress_kernel(xk_ref, xv_ref, w1_ref, pos_ref, w2_ref, o_ref, tail_ref, *, ncp):
    kc, vc = _compress_rows((xk_ref.at[0], xv_ref.at[0]), ncp, w1_ref, pos_ref, w2_ref, tail_ref)
    o_ref[0, :, 0:LANES] = kc
    o_ref[0, :, LANES:2 * LANES] = vc


def _compress_prompt(proj, w1bd, posrow, w2bd):
    bsz, t, _ = proj.shape
    ncp = t // CMP_STRIDE
    return pl.pallas_call(
        functools.partial(_compress_kernel, ncp=ncp),
        grid=(bsz,),
        in_specs=[pl.BlockSpec((1, t, LANES), lambda b: (b, 0, C_KVB // LANES)),
                  pl.BlockSpec((1, t, LANES), lambda b: (b, 0, C_KVB // LANES + 1)),
                  pl.BlockSpec((CMP_LEN, 2, LANES, LANES), lambda b: (0, 0, 0, 0)),
                  pl.BlockSpec((CMP_LEN, 2, 1, LANES), lambda b: (0, 0, 0, 0)),
                  pl.BlockSpec((2, LANES, LANES), lambda b: (0, 0, 0))],
        out_specs=pl.BlockSpec((1, ncp, 2 * LANES), lambda b: (b, 0, 0)),
        out_shape=jax.ShapeDtypeStruct((bsz, ncp, 2 * LANES), F32),
        scratch_shapes=[pltpu.VMEM((2, ncp + SUBLANES, LANES), F32)],
        compiler_params=_cparams(("parallel",)),
    )(proj, proj, w1bd, posrow, w2bd)


def _group_queries(q, g, nq):
    lane = lax.broadcasted_iota(jnp.int32, (nq, LANES), 1)
    in_g = (lane >= HEAD_DIM) if g else (lane < HEAD_DIM)
    return jnp.concatenate(
        [jnp.where(in_g, q[:, r * LANES:(r + 1) * LANES], 0.0) for r in range(R_NSA)], axis=0)


def _masked_softmax(s, mask):
    s = jnp.where(mask, s, NEG)
    m = jnp.max(s, axis=1, keepdims=True)
    p = jnp.where(mask, jnp.exp(s - m), 0.0)
    l = jnp.sum(p, axis=1, keepdims=True)
    return p / jnp.maximum(l, 1e-30)


def _select_blocks(imp, q_pos, n_blk, axis):
    nb = imp.shape[axis]
    blk = lax.broadcasted_iota(jnp.int32, imp.shape, axis)
    blk_f = blk.astype(F32)
    cur = q_pos // SEL_BLOCK
    forced = (blk == 0) | (blk == cur) | (blk == cur - 1)
    valid = blk * SEL_BLOCK <= q_pos
    score = jnp.where(forced, FORCE_SCORE, jnp.where(valid, imp, -1.0))
    score = jnp.where(blk < n_blk, score, -jnp.inf)
    sel = jnp.zeros(imp.shape, F32)
    for _ in range(SEL_TOPK):
        m = jnp.max(score, axis=axis, keepdims=True)
        first = jnp.min(jnp.where(score == m, blk_f, float(nb)), axis=axis, keepdims=True)
        pick = blk_f == first
        sel = jnp.where(pick, 1.0, sel)
        score = jnp.where(pick, -jnp.inf, score)
    return sel


def _gate_combine(sig, o_c, o_s, o_w, nq):
    lane = lax.broadcasted_iota(jnp.int32, (nq, LANES), 1)
    slabs = []
    for r in range(R_NSA):
        per_g = []
        for g in range(G_NSA):
            h = g * R_NSA + r
            rows = slice(r * nq, (r + 1) * nq)
            acc = sig[:, h:h + 1] * o_c[g][rows]
            acc = acc + sig[:, H_NSA + h:H_NSA + h + 1] * o_s[g][rows]
            acc = acc + sig[:, 2 * H_NSA + h:2 * H_NSA + h + 1] * o_w[g][rows]
            per_g.append(acc)
        slabs.append(jnp.where(lane < HEAD_DIM, per_g[0], per_g[1]))
    return jnp.concatenate(slabs, axis=1)


def _nsa_prompt_kernel(q_ref, gate_ref, kcvc_ref, selkv_ref, w0_ref, w1_ref, w2_ref, w3_ref, w4_ref,
                       tt_ref, tts_ref, tca_ref, tcb_ref, tcf_ref, mapt_ref, o_ref,
                       qg_ref, qa_ref, m_ref, acc_ref, *, ncp, n_blk):
    tq = TQ
    i = pl.program_id(1)
    q0 = i * tq
    rows = R_NSA * tq
    q = q_ref[0] * QK_SCALE
    lane = lax.broadcasted_iota(jnp.int32, (tq, LANES), 1)
    q_pos = q0 + lax.broadcasted_iota(jnp.int32, (tq, LANES), 0)
    nbp = mapt_ref.shape[0]
    q_pos_t = q0 + lax.broadcasted_iota(jnp.int32, (nbp, tq), 1)
    win_refs = (w0_ref, w1_ref, w2_ref, w3_ref, w4_ref)
    n_ctile = ncp // LANES
    jn_a = i // 16
    o_c, o_s, o_w = [], [], []

    for g in range(G_NSA):
        qg = _group_queries(q, g, tq)
        qg_ref[g] = _bf(qg)
        s = _dot3_t(qg, kcvc_ref[0, :, 0:LANES])
        n_idx = lax.broadcasted_iota(jnp.int32, (tq, ncp), 1)
        qp_c = q0 + lax.broadcasted_iota(jnp.int32, (tq, ncp), 0)
        mask_c = (n_idx * CMP_STRIDE + CMP_LEN - 1) <= qp_c
        bias_rows = []
        for r in range(R_NSA):
            h = g * R_NSA + r
            tiles = []
            for jn in range(n_ctile):
                t = jnp.where(jn == jn_a, tca_ref[0, h],
                              jnp.where(jn == jn_a - 1, tcb_ref[0, h], tcf_ref[0, h]))
                tiles.append(t)
            bias_rows.append(jnp.concatenate(tiles, axis=1) if n_ctile > 1 else tiles[0])
        bias_c = jnp.concatenate(bias_rows, axis=0)
        mask_c4 = jnp.concatenate([mask_c] * R_NSA, axis=0)
        p_c = _masked_softmax(s + bias_c, mask_c4)
        o_c.append(_dot(_bf(p_c), _bf(kcvc_ref[0, :, LANES:2 * LANES])))
        p_sum = p_c[0:tq]
        for r in range(1, R_NSA):
            p_sum = p_sum + p_c[r * tq:(r + 1) * tq]
        p_hi, p_lo = _split(p_sum)
        imp_t = _dot_t(mapt_ref[...], p_hi) + _dot_t(mapt_ref[...], p_lo)
        sel_t = _select_blocks(imp_t, q_pos_t, n_blk, axis=0)
        not_sel = _bf(1.0 - sel_t.T)
        qa_ref[g] = jnp.concatenate([_bf(qg * LOG2E), jnp.concatenate([not_sel] * R_NSA, axis=0)], axis=1)

    m_ref[...] = jnp.full((G_NSA, rows, LANES), NEG, F32)
    acc_ref[...] = jnp.zeros((G_NSA, rows, LANES), F32)
    causal4 = jnp.concatenate([(q0 + lane) <= q_pos] * R_NSA, axis=0)

    def sel_chunk(j0, nt, near, causal=False):
        width = nt * LANES
        k0 = pl.multiple_of(j0 * LANES, LANES)
        kv = selkv_ref[0, pl.ds(k0, width), :]
        v = kv[:, LANES:2 * LANES]
        v_lane = lax.broadcasted_iota(jnp.int32, (width, LANES), 1)
        key_blk = 2 * j0 + lax.broadcasted_iota(jnp.int32, (width, nbp), 0) // SEL_BLOCK
        blk_col = lax.broadcasted_iota(jnp.int32, (width, nbp), 1)
        k_aug = jnp.concatenate([_bf(kv[:, 0:LANES]),
                                 jnp.where(key_blk == blk_col, MASK_VALUE, 0.0).astype(BF16)], axis=1)
        for g in range(G_NSA):
            vt = _bf(jnp.where((v_lane < HEAD_DIM) if g else (v_lane >= HEAD_DIM), 1.0, v))
            sc = _dot_t(qa_ref[g], k_aug)
            if near:
                d = i - j0
                sc = sc + jnp.concatenate([tts_ref[d, g * R_NSA + r] for r in range(R_NSA)], axis=0)
            if causal:
                sc = jnp.where(causal4, sc, NEG)
            m_old = m_ref[g]
            m_new = jnp.maximum(m_old, jnp.max(sc, axis=1, keepdims=True))
            p = jnp.exp2(sc - (jnp.concatenate([m_new] * nt, axis=1) if nt > 1 else m_new))
            acc_ref[g] = jnp.exp2(m_old - m_new) * acc_ref[g] + _dot(_bf(p), vt)
            m_ref[g] = m_new

    sel_chunk(i, 1, True, causal=True)
    n_far = jnp.maximum(i - (N_TOK_TILES - 2), 0)

    def near_body(j, carry):
        sel_chunk(j, 1, True)
        return carry

    lax.fori_loop(n_far, i, near_body, 0)

    def far_body(c, carry):
        sel_chunk(2 * c, 2, False)
        return carry

    lax.fori_loop(0, n_far // 2, far_body, 0)

    @pl.when(n_far % 2 == 1)
    def _():
        sel_chunk(n_far - 1, 1, False)

    for g in range(G_NSA):
        acc = acc_ref[g]
        den = acc[:, 0:1] if g else acc[:, LANES - 1:LANES]
        o_s.append(acc / den)

    for g in range(G_NSA):
        s_tiles, v_tiles, m_tiles = [], [], []
        for dlt in range(WINDOW // LANES + 1):
            kv = win_refs[dlt][0]
            kt = _bf(kv[:, 0:LANES])
            v_tiles.append(_bf(kv[:, LANES:2 * LANES]))
            dist = q_pos - ((i - dlt) * LANES + lane)
            mk = (dist >= 0) & (dist < WINDOW) & (i - dlt >= 0)
            bias = jnp.concatenate([tt_ref[dlt, g * R_NSA + r] for r in range(R_NSA)], axis=0)
            s_tiles.append(_dot_t(qg_ref[g], kt) + bias)
            m_tiles.append(jnp.concatenate([mk] * R_NSA, axis=0))
        p_w = _masked_softmax(jnp.concatenate(s_tiles, axis=1), jnp.concatenate(m_tiles, axis=1))
        o_w.append(_dot(_bf(p_w), jnp.concatenate(v_tiles, axis=0)))

    sig = jax.nn.sigmoid(gate_ref[0])
    o_ref[0] = _gate_combine(sig, o_c, o_s, o_w, tq)


def _nsa_prompt(proj, kcvc, tt, tts, tc, sel_map):
    bsz, t, _ = proj.shape
    tq = TQ
    ncp = t // CMP_STRIDE
    n_blk = t // SEL_BLOCK
    map_t = sel_map.T
    nbp = map_t.shape[0]
    rows = R_NSA * tq
    nwin = WINDOW // LANES + 1
    win_specs = [
        pl.BlockSpec((1, tq, 2 * LANES),
                     functools.partial(lambda b, i, d: (b, jnp.maximum(i - d, 0), (C_KVB + 4 * LANES) // (2 * LANES)), d=d))
        for d in range(nwin)]
    far = N_CMP_TILES - 1
    return pl.pallas_call(
        functools.partial(_nsa_prompt_kernel, ncp=ncp, n_blk=n_blk),
        grid=(bsz, t // tq),
        in_specs=[pl.BlockSpec((1, tq, W_NSA), lambda b, i: (b, i, C_QB // W_NSA)),
                  pl.BlockSpec((1, tq, LANES), lambda b, i: (b, i, C_GATE // LANES)),
                  pl.BlockSpec((1, ncp, 2 * LANES), lambda b, i: (b, 0, 0)),
                  pl.BlockSpec((1, t, 2 * LANES), lambda b, i: (b, 0, (C_KVB + 2 * LANES) // (2 * LANES)))]
                 + win_specs +
                 [pl.BlockSpec((nwin, H_NSA, tq, LANES), lambda b, i: (0, 0, 0, 0)),
                  pl.BlockSpec((N_TOK_TILES - 1, H_NSA, tq, LANES), lambda b, i: (0, 0, 0, 0)),
                  pl.BlockSpec((1, H_NSA, tq, LANES), lambda b, i: (i % 16, 0, 0, 0)),
                  pl.BlockSpec((1, H_NSA, tq, LANES), lambda b, i: (jnp.minimum(i % 16 + 16, far), 0, 0, 0)),
                  pl.BlockSpec((1, H_NSA, tq, LANES), lambda b, i: (far, 0, 0, 0)),
                  pl.BlockSpec((nbp, ncp), lambda b, i: (0, 0))],
        out_specs=pl.BlockSpec((1, tq, W_NSA), lambda b, i: (b, i, 0)),
        out_shape=jax.ShapeDtypeStruct((bsz, t, W_NSA), F32),
        scratch_shapes=[pltpu.VMEM((G_NSA, rows, LANES), BF16),
                        pltpu.VMEM((G_NSA, rows, LANES + nbp), BF16),
                        pltpu.VMEM((G_NSA, rows, LANES), F32),
                        pltpu.VMEM((G_NSA, rows, LANES), F32)],
        compiler_params=_cparams(("parallel", "arbitrary")),
    )(proj, proj, kcvc, proj, *([proj] * nwin), tt[:nwin], tts, tc, tc, tc, map_t)


def _nsa_sample_kernel(pt_ref, p_ref, win_ref, cache_ref, w1_ref, pos_ref, w2_ref, map_ref,
                       bsel_ref, bcmp_ref, bwin_ref, o_ref, buf, sem, tail_ref, wbuf,
                       *, layer_base, n_pages, nq, n_blk):
    b = pl.program_id(0)
    past = n_pages * PAGE_SIZE
    tk = past + LANES
    ncp = past // CMP_STRIDE
    nbp = map_ref.shape[1]
    rows = R_NSA * nq
    wlen = win_ref.shape[1]
    wpad = wbuf.shape[0]

    n_slab = buf.shape[0]

    def page_copy(j, s):
        return pltpu.make_async_copy(
            cache_ref.at[layer_base + pt_ref[b, j], :, pl.ds(s * LANES, LANES)],
            buf.at[s, pl.ds(pl.multiple_of(j * PAGE_SIZE, PAGE_SIZE), PAGE_SIZE)], sem)

    def start(j, c):
        for s in range(n_slab):
            page_copy(j, s).start()
        return c

    def wait(j, c):
        for s in range(n_slab):
            page_copy(j, s).wait()
        return c

    lax.fori_loop(0, n_pages, start, 0)

    new_nsa = p_ref[0, :, C_KVB - C_QB:C_KVB - C_QB + 4 * LANES]
    for s in range(n_slab):
        buf[s, past:past + nq, :] = new_nsa[:, s * LANES:(s + 1) * LANES]
        buf[s, past + nq:tk, :] = jnp.zeros((LANES - nq, LANES), F32)
    wbuf[0:wlen, :] = win_ref[0]
    wbuf[wlen:wlen + nq, :] = p_ref[0, :, C_KVB - C_QB + 4 * LANES:C_KVB - C_QB + 6 * LANES]
    wbuf[wlen + nq:wpad, :] = jnp.zeros((wpad - wlen - nq, 2 * LANES), F32)

    q = p_ref[0, :, 0:W_NSA] * QK_SCALE
    sig = jax.nn.sigmoid(p_ref[0, :, C_GATE - C_QB:C_GATE - C_QB + LANES])
    q_pos_nb = past + lax.broadcasted_iota(jnp.int32, (nq, nbp), 0)

    lax.fori_loop(0, n_pages, wait, 0)

    kc, vc = _compress_rows((buf.at[0], buf.at[1]), ncp, w1_ref, pos_ref, w2_ref, tail_ref)
    n_idx = lax.broadcasted_iota(jnp.int32, (rows, ncp), 1)
    mask_c = n_idx < ncp - 1
    tok = lax.broadcasted_iota(jnp.int32, (rows, tk), 1)
    qi_tok = lax.broadcasted_iota(jnp.int32, (rows, tk), 0) % nq
    causal = tok <= past + qi_tok
    kk = lax.broadcasted_iota(jnp.int32, (rows, wpad), 1)
    dist_w = wlen + (lax.broadcasted_iota(jnp.int32, (rows, wpad), 0) % nq) - kk
    mask_w = (dist_w >= 0) & (dist_w < WINDOW) & (kk < wlen + nq)
    blk_row = lax.broadcasted_iota(jnp.int32, (nbp, tk), 0)
    blk_tok = lax.broadcasted_iota(jnp.int32, (nbp, tk), 1) // SEL_BLOCK
    expand = jnp.where(blk_row == blk_tok, 1.0, 0.0).astype(BF16)

    o_c, o_s, o_w = [], [], []
    for g in range(G_NSA):
        qg = _group_queries(q, g, nq)
        qg_bf = _bf(qg)
        brow = slice(g * rows, (g + 1) * rows)
        s = _dot3_t(qg, kc) + bcmp_ref[brow, :]
        p_c = _masked_softmax(s, mask_c)
        o_c.append(_dot(_bf(p_c), _bf(vc)))
        p_sum = p_c[0:nq]
        for r in range(1, R_NSA):
            p_sum = p_sum + p_c[r * nq:(r + 1) * nq]
        imp = _dot2_exact_rhs(p_sum, map_ref[...])
        sel = _select_blocks(imp, q_pos_nb, n_blk, axis=1)
        mk = _dot(_bf(jnp.concatenate([sel] * R_NSA, axis=0)), expand) > 0.5
        mk = jnp.logical_and(mk, causal)
        s = _dot_t(qg_bf, _bf(buf[2])) + bsel_ref[brow, :]
        p_s = _masked_softmax(s, mk)
        o_s.append(_dot(_bf(p_s), _bf(buf[3])))
        s = _dot_t(qg_bf, _bf(wbuf[:, 0:LANES])) + bwin_ref[brow, :]
        p_w = _masked_softmax(s, mask_w)
        o_w.append(_dot(_bf(p_w), _bf(wbuf[:, LANES:2 * LANES])))

    o_ref[0] = _gate_combine(sig, o_c, o_s, o_w, nq)


def _nsa_sample(proj_s, win_cache, cache_pages, page_table, w1bd, posrow, w2bd, sel_map,
                bsel, bcmp, bwin, layer, nq):
    dbsz, n_pages = page_table.shape
    n_pool = cache_pages.shape[0] // (win_cache.shape[0] // dbsz)
    past = n_pages * PAGE_SIZE
    tk = past + LANES
    ncp = past // CMP_STRIDE
    wlen = win_cache.shape[1]
    wpad = wlen + LANES
    n_blk = -(-(past + nq) // SEL_BLOCK)
    kern = functools.partial(_nsa_sample_kernel, layer_base=layer * n_pool, n_pages=n_pages,
                             nq=nq, n_blk=n_blk)
    const2 = lambda b, pt: (0, 0)
    grid_spec = pltpu.PrefetchScalarGridSpec(
        num_scalar_prefetch=1,
        grid=(dbsz,),
        in_specs=[pl.BlockSpec((1, nq, N_PROJ - C_QB), lambda b, pt: (0, b, 1)),
                  pl.BlockSpec((1, wlen, 2 * LANES), lambda b, pt: (layer * dbsz + b, 0, 0)),
                  pl.BlockSpec(memory_space=pl.ANY),
                  pl.BlockSpec((CMP_LEN, 2, LANES, LANES), lambda b, pt: (0, 0, 0, 0)),
                  pl.BlockSpec((CMP_LEN, 2, 1, LANES), lambda b, pt: (0, 0, 0, 0)),
                  pl.BlockSpec((2, LANES, LANES), lambda b, pt: (0, 0, 0)),
                  pl.BlockSpec(sel_map.shape, const2),
                  pl.BlockSpec(bsel.shape, const2),
                  pl.BlockSpec(bcmp.shape, const2),
                  pl.BlockSpec(bwin.shape, const2)],
        out_specs=pl.BlockSpec((1, nq, W_NSA), lambda b, pt: (0, b, 0)),
        scratch_shapes=[pltpu.VMEM((4, tk, LANES), F32),
                        pltpu.SemaphoreType.DMA(()),
                        pltpu.VMEM((2, ncp + SUBLANES, LANES), F32),
                        pltpu.VMEM((wpad, 2 * LANES), F32)],
    )
    return pl.pallas_call(
        kern, grid_spec=grid_spec,
        out_shape=jax.ShapeDtypeStruct((1, dbsz * nq, W_NSA), F32),
        compiler_params=_cparams(("arbitrary",)),
    )(page_table, proj_s, win_cache, cache_pages, w1bd, posrow, w2bd, sel_map, bsel, bcmp, bwin)


def _out_kernel(y_ref, oa_ref, ob_ref, za_ref, zb_ref, gate_ref, norm_ref, w_ref, g_ref, b_ref, o_ref,
                *, alpha):
    def gated_norm(o, z, nrm):
        ms = jnp.mean(o * o, axis=-1, keepdims=True)
        return o * lax.rsqrt(ms + RMS_EPS) * nrm * (z * jax.nn.sigmoid(z))

    ua = gated_norm(oa_ref[0], za_ref[0], norm_ref[:, 0:W_SB])
    ub = gated_norm(ob_ref[0], zb_ref[0], norm_ref[:, W_SB:D_MODEL])
    u = _bf(jnp.concatenate([ua, ub], axis=1))
    mixed = _dot(u, w_ref[...])
    x = alpha * y_ref[0] + gate_ref[0] * mixed
    mu = jnp.mean(x, axis=-1, keepdims=True)
    xc = x - mu
    var = jnp.mean(xc * xc, axis=-1, keepdims=True)
    o_ref[0] = xc * lax.rsqrt(var + LN_EPS) * g_ref[...] + b_ref[...]


def _out_stage(y, o_a, o_b, proj, gate, norm_p, w_out_bf, ln_g, ln_b, alpha):
    bsz, t, d = y.shape
    tm = min(256, t)
    if gate.shape[1] == 1:
        gate_spec = pl.BlockSpec((1, 1, d), lambda b, i: (b, 0, 0))
    else:
        gate_spec = pl.BlockSpec((1, tm, d), lambda b, i: (b, i, 0))
    row = lambda w: pl.BlockSpec((1, w), lambda b, i: (0, 0))
    return pl.pallas_call(
        functools.partial(_out_kernel, alpha=alpha),
        grid=(bsz, t // tm),
        in_specs=[pl.BlockSpec((1, tm, d), lambda b, i: (b, i, 0)),
                  pl.BlockSpec((1, tm, W_SB), lambda b, i: (b, i, 0)),
                  pl.BlockSpec((1, tm, W_NSA), lambda b, i: (b, i, 0)),
                  pl.BlockSpec((1, tm, W_SB), lambda b, i: (b, i, C_ZA // W_SB)),
                  pl.BlockSpec((1, tm, W_NSA), lambda b, i: (b, i, C_ZB // W_NSA)),
                  gate_spec, row(d),
                  pl.BlockSpec((d, d), lambda b, i: (0, 0)),
                  row(d), row(d)],
        out_specs=pl.BlockSpec((1, tm, d), lambda b, i: (b, i, 0)),
        out_shape=jax.ShapeDtypeStruct((bsz, t, d), F32),
        compiler_params=_cparams(("parallel", "parallel")),
    )(y, o_a, o_b, proj, proj, gate, norm_p.reshape(1, d), w_out_bf, ln_g.reshape(1, d), ln_b.reshape(1, d))


def _rel_bucket(dist):
    n = jnp.maximum(dist, 0)
    nf = jnp.maximum(n, 1).astype(F32)
    large = REL_MAX_EXACT + (jnp.log(nf / REL_MAX_EXACT) / math.log(REL_MAX_DIST / REL_MAX_EXACT)
                             * (N_BUCKETS - REL_MAX_EXACT)).astype(jnp.int32)
    large = jnp.minimum(large, N_BUCKETS - 1)
    return jnp.where(n < REL_MAX_EXACT, n, large)


def _nsa_perm(x, axis):
    shp = x.shape
    x = x.reshape(shp[:axis] + (G_NSA, R_NSA, HEAD_DIM) + shp[axis + 1:])
    x = jnp.swapaxes(x, axis, axis + 1)
    return x.reshape(shp)


def _bias_tile_kernel(base_ref, step_ref, thr_ref, rb_ref, o_ref):
    t = pl.program_id(0)
    shape = o_ref.shape[2:]
    dist = (base_ref[t] + lax.broadcasted_iota(jnp.int32, shape, 0)
            - step_ref[t] * lax.broadcasted_iota(jnp.int32, shape, 1))
    for h in range(H_NSA):
        val = jnp.full(shape, rb_ref[0, h], F32)
        for b in range(1, N_BUCKETS):
            val = jnp.where(dist >= thr_ref[b], rb_ref[b, h], val)
        o_ref[0, h] = val


def _bias_tiles(base, step, thr, rel_bias, rows):
    n = len(base)
    grid_spec = pltpu.PrefetchScalarGridSpec(
        num_scalar_prefetch=3,
        grid=(n,),
        in_specs=[pl.BlockSpec(memory_space=pltpu.SMEM)],
        out_specs=pl.BlockSpec((1, H_NSA, rows, LANES), lambda t, *_: (t, 0, 0, 0)),
    )
    return pl.pallas_call(
        _bias_tile_kernel, grid_spec=grid_spec,
        out_shape=jax.ShapeDtypeStruct((n, H_NSA, rows, LANES), F32),
        compiler_params=_cparams(("parallel",)),
    )(jnp.asarray(base, jnp.int32), jnp.asarray(step, jnp.int32), thr, rel_bias)


def _bias_tables(rel_bias, nq, past, wlen):
    bucket = _rel_bucket(jnp.arange(BIAS_DMAX))
    thr = jnp.sum(bucket[None, :] < jnp.arange(N_BUCKETS)[:, None], axis=1).astype(jnp.int32)
    base = [LANES * d for d in range(N_TOK_TILES)] + [LANES * o - (CMP_LEN - 1) for o in range(N_CMP_TILES)]
    step = [1] * N_TOK_TILES + [CMP_STRIDE] * N_CMP_TILES
    tiles = _bias_tiles(base, step, thr, rel_bias, TQ)
    tt, tc = tiles[:N_TOK_TILES], tiles[N_TOK_TILES:]
    n_key, n_cmp, n_win = past // LANES + 1, past // (CMP_STRIDE * LANES), wlen // LANES + 1
    base = ([past - LANES * k for k in range(n_key)]
            + [past - CMP_STRIDE * LANES * j - (CMP_LEN - 1) for j in range(n_cmp)]
            + [wlen - LANES * k for k in range(n_win)])
    step = [1] * n_key + [CMP_STRIDE] * n_cmp + [1] * n_win
    tiles = _bias_tiles(base, step, thr, rel_bias, nq)
    rows = lambda x: jnp.transpose(x, (1, 2, 0, 3)).reshape(H_NSA * nq, x.shape[0] * LANES)
    bsel, bcmp, bwin = rows(tiles[:n_key]), rows(tiles[n_key:n_key + n_cmp]), rows(tiles[n_key + n_cmp:])
    return tt, tc, bsel, bcmp, bwin


def _selection_map(ncp, n_c, n_blk):
    nbp = -(-n_blk // LANES) * LANES
    cs = np.arange(ncp)[:, None] * CMP_STRIDE
    ss = np.arange(nbp)[None, :] * SEL_BLOCK
    ov = np.minimum(cs + CMP_LEN, ss + SEL_BLOCK) - np.maximum(cs, ss)
    m = np.clip(ov, 0, None).astype(np.float32) / CMP_LEN
    m = m * (np.arange(ncp)[:, None] < n_c) * (np.arange(nbp)[None, :] < n_blk)
    return jnp.asarray(m, dtype=BF16)


def _block_diag2(w):
    z = jnp.zeros_like(w)
    return jnp.concatenate([jnp.concatenate([w, z], axis=-1), jnp.concatenate([z, w], axis=-1)], axis=-2)


def kernel(x_prompt, x_sample, cache_sb_kv, cache_nsa_kv, cache_win_kv, page_table, c_prompt, c_sample,
           ln_in_g, ln_in_b, w_ada, b_ada, w_in, w_cmp1, w_cmp2, pos_cmp, norm_grp, w_out, ln_g, ln_b,
           rel_bias):
    bsz, seq, d = x_prompt.shape
    dbsz, nq, _ = x_sample.shape
    depth = w_ada.shape[0]
    n_pool = cache_sb_kv.shape[1]
    n_pages = page_table.shape[1]
    past = n_pages * PAGE_SIZE
    wlen = cache_win_kv.shape[2]
    alpha = (2 * depth) ** 0.25
    assert d == D_MODEL and seq % (16 * TQ) == 0 and past % (16 * TQ) == 0 and wlen == WINDOW
    assert seq // SEL_BLOCK >= SEL_TOPK and nq == SUBLANES

    w_qb = _nsa_perm(w_in[:, :, 2048:2560], 2)
    w_zb = _nsa_perm(w_in[:, :, 3352:3864], 2)
    pad = lambda n: jnp.zeros((depth, d, n), F32)
    w_in_p = _bf(jnp.concatenate(
        [w_in[:, :, 0:2048], w_qb, w_in[:, :, 2560:3328], w_in[:, :, 3328:3352], pad(C_ZB - C_GATE - 3 * H_NSA),
         w_zb], axis=2))
    norm_p = jnp.concatenate([norm_grp[:, :W_SB], _nsa_perm(norm_grp[:, W_SB:], 1)], axis=1)
    w_out_p = _bf(jnp.concatenate([w_out[:, :W_SB], _nsa_perm(w_out[:, W_SB:], 1)], axis=1))
    w1bd = _bf(jnp.swapaxes(_block_diag2(w_cmp1), 1, 2))
    w2bd = _bf(_block_diag2(w_cmp2))
    posrow = jnp.swapaxes(jnp.concatenate([pos_cmp, pos_cmp], axis=-1), 1, 2)[:, :, :, None, :]
    tt, tc, bsel, bcmp, bwin = _bias_tables(rel_bias, nq, past, wlen)
    tts = (tt[:N_TOK_TILES - 1] - tt[N_TOK_TILES - 1:]) * LOG2E
    map_p = _selection_map(seq // CMP_STRIDE, (seq - CMP_LEN) // CMP_STRIDE + 1, seq // SEL_BLOCK)
    n_blk_s = -(-(past + nq) // SEL_BLOCK)
    map_s = _selection_map(past // CMP_STRIDE, (past + nq - CMP_LEN) // CMP_STRIDE + 1, n_blk_s)

    n_c = bsz + dbsz
    c_rows = -(-n_c // SUBLANES) * SUBLANES
    c_all = jnp.concatenate([c_prompt, c_sample, jnp.zeros((c_rows - n_c, d), F32)], axis=0)
    mod = _ada_mod(c_all, w_ada, b_ada)

    y_p = _layer_norm(x_prompt.reshape(bsz * seq, d), ln_in_g, ln_in_b).reshape(bsz, seq, d)
    y_s = _layer_norm(x_sample.reshape(dbsz * nq, d), ln_in_g, ln_in_b).reshape(1, dbsz * nq, d)

    sb_pages = cache_sb_kv.reshape(depth * n_pool, PAGE_SIZE, 2 * W_SB)
    nsa_pages = cache_nsa_kv.reshape(depth * n_pool, PAGE_SIZE, 4 * KV_NSA)
    win_cache = cache_win_kv.reshape(depth * dbsz, wlen, 2 * KV_NSA)

    p_sb, p_nsa, p_win, s_sb, s_nsa, s_win = [], [], [], [], [], []
    for l in range(depth):
        mp = mod[l, :bsz].reshape(bsz, 1, 3 * d)
        proj = _in_proj(y_p, mp[:, :, 0:d], mp[:, :, d:2 * d], w_in_p[l])
        o_a = _sb_prompt(proj)
        kcvc = _compress_prompt(proj, w1bd[l], posrow[l], w2bd[l])
        o_b = _nsa_prompt(proj, kcvc, tt, tts, tc, map_p)
        p_sb.append(proj[:, :, C_KA:C_ZA].reshape(bsz, seq, 2, H_SB, HEAD_DIM))
        p_nsa.append(proj[:, :, C_KVB:C_KVB + 4 * KV_NSA].reshape(bsz, seq, 4, G_NSA, HEAD_DIM))
        p_win.append(proj[:, seq - min(WINDOW, seq):, C_KVB + 4 * KV_NSA:C_KVB + 6 * KV_NSA]
                     .reshape(bsz, min(WINDOW, seq), 2, G_NSA, HEAD_DIM))
        y_p = _out_stage(y_p, o_a, o_b, proj, mp[:, :, 2 * d:3 * d], norm_p[l], w_out_p[l], ln_g[l], ln_b[l],
                         alpha)
        ms = jnp.repeat(mod[l, bsz:bsz + dbsz], nq, axis=0).reshape(1, dbsz * nq, 3 * d)
        proj_s = _in_proj(y_s, ms[:, :, 0:d], ms[:, :, d:2 * d], w_in_p[l])
        o_a = _sb_sample(proj_s, sb_pages, page_table, l * n_pool, nq)
        o_b = _nsa_sample(proj_s, win_cache, nsa_pages, page_table, w1bd[l], posrow[l], w2bd[l], map_s,
                          bsel, bcmp, bwin, l, nq)
        ps = proj_s.reshape(dbsz, nq, N_PROJ)
        s_sb.append(ps[:, :, C_KA:C_ZA].reshape(dbsz, nq, 2, H_SB, HEAD_DIM))
        s_nsa.append(ps[:, :, C_KVB:C_KVB + 4 * KV_NSA].reshape(dbsz, nq, 4, G_NSA, HEAD_DIM))
        new_win = ps[:, :, C_KVB + 4 * KV_NSA:C_KVB + 6 * KV_NSA].reshape(dbsz, nq, 2, G_NSA, HEAD_DIM)
        win_all = jnp.concatenate([cache_win_kv[l], new_win], axis=1)
        s_win.append(win_all[:, win_all.shape[1] - min(WINDOW, past + nq):])
        y_s = _out_stage(y_s, o_a, o_b, proj_s, ms[:, :, 2 * d:3 * d], norm_p[l], w_out_p[l], ln_g[l],
                         ln_b[l], alpha)

    return (y_p, y_s.reshape(dbsz, nq, d), jnp.stack(p_sb), jnp.stack(p_nsa), jnp.stack(p_win),
            jnp.stack(s_sb), jnp.stack(s_nsa), jnp.stack(s_win))
```

```python
import functools
import math

import numpy as np
import jax
import jax.numpy as jnp
from jax import lax
from jax.experimental import pallas as pl
from jax.experimental.pallas import tpu as pltpu

F32 = jnp.float32
BF16 = jnp.bfloat16

D_MODEL = 1024
HEAD_DIM = 64
W_SB = D_MODEL // 2
W_NSA = D_MODEL - W_SB
H_SB = W_SB // HEAD_DIM
H_NSA = W_NSA // HEAD_DIM
G_NSA = 2
R_NSA = H_NSA // G_NSA
KV_NSA = G_NSA * HEAD_DIM
CMP_LEN = 32
CMP_STRIDE = 16
SEL_BLOCK = 64
SEL_TOPK = 16
WINDOW = 512
PAGE_SIZE = 128
N_BUCKETS = 32
REL_MAX_EXACT = N_BUCKETS // 2
REL_MAX_DIST = 1024
LN_EPS = 1e-5
RMS_EPS = 1e-6
NEG = -1e30
MASK_VALUE = -1e30
FORCE_SCORE = 1e4
QK_SCALE = HEAD_DIM ** -0.5
LOG2E = math.log2(math.e)

LANES = 128
SUBLANES = 8
VMEM_LIMIT = 56 * 1024 * 1024

C_QA, C_KA, C_VA, C_ZA, C_QB, C_KVB = 0, 512, 1024, 1536, 2048, 2560
C_GATE, C_ZB, N_PROJ = 3328, 3584, 4096
SB_DEAD = -104.0
BIAS_DMAX = 1024
N_TOK_TILES = 9
N_CMP_TILES = 24
TQ = 128


def _bf(x):
    return x.astype(BF16)


def _dot(a, b):
    return jnp.dot(a, b, preferred_element_type=F32)


def _dot_t(a, b):
    return lax.dot_general(a, b, (((1,), (1,)), ((), ())), preferred_element_type=F32)


def _split(x):
    hi = _bf(x)
    lo = _bf(x - hi.astype(F32))
    return hi, lo


def _dot3_t(a, b):
    ah, al = _split(a)
    bh, bl = _split(b)
    return _dot_t(ah, bh) + _dot_t(ah, bl) + _dot_t(al, bh)


def _dot3(a, b):
    ah, al = _split(a)
    bh, bl = _split(b)
    return _dot(ah, bh) + _dot(ah, bl) + _dot(al, bh)


def _dot2_exact_rhs(a, b_bf):
    ah, al = _split(a)
    return _dot(ah, b_bf) + _dot(al, b_bf)


def _softplus(z):
    return jnp.maximum(z, 0.0) + jnp.log1p(jnp.exp(-jnp.abs(z)))


def _cparams(sem):
    return pltpu.CompilerParams(dimension_semantics=sem, vmem_limit_bytes=VMEM_LIMIT)


def _ln_kernel(x_ref, g_ref, b_ref, o_ref):
    x = x_ref[...]
    mu = jnp.mean(x, axis=-1, keepdims=True)
    xc = x - mu
    var = jnp.mean(xc * xc, axis=-1, keepdims=True)
    o_ref[...] = xc * lax.rsqrt(var + LN_EPS) * g_ref[...] + b_ref[...]


def _layer_norm(x2d, g, b):
    rows, d = x2d.shape
    tm = min(512, rows)
    return pl.pallas_call(
        _ln_kernel,
        grid=(rows // tm,),
        in_specs=[pl.BlockSpec((tm, d), lambda i: (i, 0)),
                  pl.BlockSpec((1, d), lambda i: (0, 0)),
                  pl.BlockSpec((1, d), lambda i: (0, 0))],
        out_specs=pl.BlockSpec((tm, d), lambda i: (i, 0)),
        out_shape=jax.ShapeDtypeStruct((rows, d), F32),
        compiler_params=_cparams(("parallel",)),
    )(x2d, g.reshape(1, d), b.reshape(1, d))


def _mod_kernel(c_ref, w_ref, b_ref, o_ref):
    c = c_ref[...]
    s = c * jax.nn.sigmoid(c)
    o_ref[0] = _dot3(s, w_ref[0]) + b_ref[0]


def _ada_mod(c_all, w_ada, b_ada):
    depth, d, n3 = w_ada.shape
    rows = c_all.shape[0]
    tn = 1024
    return pl.pallas_call(
        _mod_kernel,
        grid=(depth, n3 // tn),
        in_specs=[pl.BlockSpec((rows, d), lambda l, j: (0, 0)),
                  pl.BlockSpec((1, d, tn), lambda l, j: (l, 0, j)),
                  pl.BlockSpec((1, 1, tn), lambda l, j: (l, 0, j))],
        out_specs=pl.BlockSpec((1, rows, tn), lambda l, j: (l, 0, j)),
        out_shape=jax.ShapeDtypeStruct((depth, rows, n3), F32),
        compiler_params=_cparams(("parallel", "parallel")),
    )(c_all, w_ada, b_ada.reshape(depth, 1, n3))


def _inproj_kernel(y_ref, sh_ref, sc_ref, w_ref, o_ref):
    h = y_ref[0] * (1.0 + sc_ref[0]) + sh_ref[0]
    hb = _bf(h)
    chunk = 512
    for c0 in range(0, N_PROJ, chunk):
        o_ref[0, :, c0:c0 + chunk] = _dot(hb, w_ref[:, c0:c0 + chunk])


def _in_proj(y, shift, scale, w_bf):
    bsz, t, d = y.shape
    tm = min(256, t)
    tmod = shift.shape[1]
    if tmod == 1:
        mod_spec = pl.BlockSpec((1, 1, d), lambda b, i: (b, 0, 0))
    else:
        mod_spec = pl.BlockSpec((1, tm, d), lambda b, i: (b, i, 0))
    return pl.pallas_call(
        _inproj_kernel,
        grid=(bsz, t // tm),
        in_specs=[pl.BlockSpec((1, tm, d), lambda b, i: (b, i, 0)), mod_spec, mod_spec,
                  pl.BlockSpec((d, N_PROJ), lambda b, i: (0, 0))],
        out_specs=pl.BlockSpec((1, tm, N_PROJ), lambda b, i: (b, i, 0)),
        out_shape=jax.ShapeDtypeStruct((bsz, t, N_PROJ), F32),
        compiler_params=_cparams(("parallel", "parallel")),
    )(y, shift, scale, w_bf)


def _sb_tile(qh, kt, vt, mask, carry, tri):
    a, carry = _sb_weights(_dot_t(qh, kt), mask, carry, tri)
    return _dot(_bf(a), vt), carry


def _sb_weights(z, mask, carry, tri):
    lk = -_softplus(z)
    if mask is not None:
        lk = jnp.where(mask, lk, 0.0)
    after = []
    for blk in reversed(range(z.shape[1] // LANES)):
        lk_b = lk[:, blk * LANES:(blk + 1) * LANES]
        hi, lo = _split(lk_b)
        after.insert(0, _dot(hi, tri) + _dot(lo, tri) + carry)
        carry = carry + jnp.sum(lk_b, axis=1, keepdims=True)
    after = jnp.concatenate(after, axis=1) if len(after) > 1 else after[0]
    a = jnp.exp(z + lk + after)
    if mask is not None:
        a = jnp.where(mask, a, 0.0)
    return a, carry


def _tri_newer():
    r = lax.broadcasted_iota(jnp.int32, (LANES, LANES), 0)
    c = lax.broadcasted_iota(jnp.int32, (LANES, LANES), 1)
    return jnp.where(r > c, 1.0, 0.0).astype(BF16)


def _sb_prompt_kernel(q_ref, k_ref, v_ref, o_ref, *, tq):
    i = pl.program_id(2)
    q0 = i * tq
    q = q_ref[0] * QK_SCALE
    lane = lax.broadcasted_iota(jnp.int32, (tq, LANES), 1)
    row_pos = q0 + lax.broadcasted_iota(jnp.int32, (tq, LANES), 0)
    tri = _tri_newer()
    qh = _bf(jnp.concatenate([jnp.where(lane < HEAD_DIM, q, 0.0), jnp.where(lane >= HEAD_DIM, q, 0.0)], axis=0))
    def chunk(c, mask, carry):
        k0 = pl.multiple_of(c * tq, tq)
        kt = _bf(k_ref[0, pl.ds(k0, tq), :])
        vt = _bf(v_ref[0, pl.ds(k0, tq), :])
        return _sb_tile(qh, kt, vt, mask, carry, tri)

    key_pos = q0 + lax.broadcasted_iota(jnp.int32, (2 * tq, tq), 1)
    qry_pos = q0 + lax.broadcasted_iota(jnp.int32, (2 * tq, tq), 0) % tq
    acc, carry = chunk(i, key_pos < qry_pos, jnp.zeros((2 * tq, LANES), F32))

    def cond(s):
        c, alive, _, _ = s
        return jnp.logical_and(c >= 0, alive)

    def body(s):
        c, _, carry, acc = s
        contrib, carry = chunk(c, None, carry)
        return c - 1, jnp.max(carry) > SB_DEAD, carry, acc + contrib

    _, _, _, acc = lax.while_loop(cond, body, (i - 1, jnp.max(carry) > SB_DEAD, carry, acc))
    o_ref[0] = jnp.where(lane < HEAD_DIM, acc[0:tq], acc[tq:2 * tq])


def _sb_prompt(proj, tq=256):
    bsz, t, _ = proj.shape
    tq = min(tq, t)
    npair = W_SB // LANES
    return pl.pallas_call(
        functools.partial(_sb_prompt_kernel, tq=tq),
        grid=(bsz, npair, t // tq),
        in_specs=[pl.BlockSpec((1, tq, LANES), lambda b, p, i: (b, i, C_QA // LANES + p)),
                  pl.BlockSpec((1, t, LANES), lambda b, p, i: (b, 0, C_KA // LANES + p)),
                  pl.BlockSpec((1, t, LANES), lambda b, p, i: (b, 0, C_VA // LANES + p))],
        out_specs=pl.BlockSpec((1, tq, LANES), lambda b, p, i: (b, i, p)),
        out_shape=jax.ShapeDtypeStruct((bsz, t, W_SB), F32),
        compiler_params=_cparams(("parallel", "parallel", "arbitrary")),
    )(proj, proj, proj)


def _sb_sample_kernel(pt_ref, p_ref, last_ref, cache_ref, o_ref, buf, sem, carry_ref, acc_ref,
                      *, layer_base, n_pages, nq):
    b = pl.program_id(0)
    rows = H_SB * nq
    q = p_ref[0, :, C_QA:C_QA + W_SB] * QK_SCALE
    lane_c = lax.broadcasted_iota(jnp.int32, (rows, W_SB), 1)
    row_c = lax.broadcasted_iota(jnp.int32, (rows, W_SB), 0)
    own = (lane_c // HEAD_DIM) == (row_c // nq)
    qbd = _bf(jnp.where(own, jnp.concatenate([q] * H_SB, axis=0), 0.0))
    tri = _tri_newer()
    key = lax.broadcasted_iota(jnp.int32, (rows, LANES), 1)
    qi = lax.broadcasted_iota(jnp.int32, (rows, LANES), 0) % nq

    pad = jnp.zeros((LANES - nq, W_SB), F32)
    k_new = _bf(jnp.concatenate([p_ref[0, :, C_KA:C_KA + W_SB], pad], axis=0))
    v_new = _bf(jnp.concatenate([p_ref[0, :, C_VA:C_VA + W_SB], pad], axis=0))
    contrib, carry = _sb_tile(qbd, k_new, v_new, key < qi, jnp.zeros((rows, LANES), F32), tri)
    carry_ref[...] = carry
    acc_ref[...] = contrib

    def page_tile(kv_ref):
        kt_t = _bf(kv_ref[0].reshape(W_SB, PAGE_SIZE))
        vt_t = _bf(kv_ref[1].reshape(W_SB, PAGE_SIZE))
        a, carry = _sb_weights(_dot(qbd, kt_t), None, carry_ref[...], tri)
        acc_ref[...] += _dot_t(_bf(a), vt_t)
        carry_ref[...] = carry

    @pl.when(jnp.max(carry_ref[...]) > SB_DEAD)
    def _():
        page_tile(last_ref.at[0])

    def cond(c):
        j, alive = c
        return jnp.logical_and(j >= 0, alive)

    def body(c):
        j, _ = c
        page = layer_base + pt_ref[b, j]
        cp = pltpu.make_async_copy(cache_ref.at[page], buf, sem)
        cp.start()
        cp.wait()
        page_tile(buf)
        return j - 1, jnp.max(carry_ref[...]) > SB_DEAD

    lax.while_loop(cond, body, (n_pages - 2, jnp.max(carry_ref[...]) > SB_DEAD))

    acc = jnp.where(own, acc_ref[...], 0.0)
    o = acc[0:nq]
    for h in range(1, H_SB):
        o = o + acc[h * nq:(h + 1) * nq]
    o_ref[0] = o


def _sb_sample(proj_s, cache_pages, page_table, layer_base, nq):
    dbsz, n_pages = page_table.shape
    rows = H_SB * nq
    page_shape = cache_pages.shape[1:]
    kern = functools.partial(_sb_sample_kernel, layer_base=layer_base, n_pages=n_pages, nq=nq)
    grid_spec = pltpu.PrefetchScalarGridSpec(
        num_scalar_prefetch=1,
        grid=(dbsz,),
        in_specs=[pl.BlockSpec((1, nq, C_QB), lambda b, pt: (0, b, 0)),
                  pl.BlockSpec((1,) + page_shape,
                               lambda b, pt: (layer_base + pt[b, n_pages - 1], 0, 0, 0, 0)),
                  pl.BlockSpec(memory_space=pl.ANY)],
        out_specs=pl.BlockSpec((1, nq, W_SB), lambda b, pt: (0, b, 0)),
        scratch_shapes=[pltpu.VMEM(page_shape, F32),
                        pltpu.SemaphoreType.DMA(()),
                        pltpu.VMEM((rows, LANES), F32),
                        pltpu.VMEM((rows, W_SB), F32)],
    )
    return pl.pallas_call(
        kern, grid_spec=grid_spec,
        out_shape=jax.ShapeDtypeStruct((1, dbsz * nq, W_SB), F32),
        compiler_params=_cparams(("arbitrary",)),
    )(page_table, proj_s, cache_pages, cache_pages)


def _compress_rows(x_refs, ncp, w1_ref, pos_ref, w2_ref, tail_ref):
    outs = []
    for c, x_ref in enumerate(x_refs):
        head = jnp.zeros((ncp, LANES), F32)
        tail = jnp.zeros((ncp, LANES), F32)
        for l in range(CMP_STRIDE):
            r = x_ref[pl.ds(l, ncp, stride=CMP_STRIDE), :]
            head = head + _dot(_bf(r + pos_ref[l, c]), w1_ref[l, c])
            tail = tail + _dot(_bf(r + pos_ref[CMP_STRIDE + l, c]), w1_ref[CMP_STRIDE + l, c])
        tail_ref[c, 0:ncp, :] = tail
        tail_ref[c, ncp:ncp + SUBLANES, :] = jnp.zeros((SUBLANES, LANES), F32)
        hid = jax.nn.gelu(head + tail_ref[c, pl.ds(1, ncp), :])
        outs.append(_dot(_bf(hid), w2_ref[c]))
    return outs


def _compress_kernel(xk_ref, xv_ref, w1_ref, pos_ref, w2_ref, o_ref, tail_ref, *, ncp):
    kc, vc = _compress_rows((xk_ref.at[0], xv_ref.at[0]), ncp, w1_ref, pos_ref, w2_ref, tail_ref)
    o_ref[0, :, 0:LANES] = kc
    o_ref[0, :, LANES:2 * LANES] = vc


def _compress_prompt(proj, w1bd, posrow, w2bd):
    bsz, t, _ = proj.shape
    ncp = t // CMP_STRIDE
    return pl.pallas_call(
        functools.partial(_compress_kernel, ncp=ncp),
        grid=(bsz,),
        in_specs=[pl.BlockSpec((1, t, LANES), lambda b: (b, 0, C_KVB // LANES)),
                  pl.BlockSpec((1, t, LANES), lambda b: (b, 0, C_KVB // LANES + 1)),
                  pl.BlockSpec((CMP_LEN, 2, LANES, LANES), lambda b: (0, 0, 0, 0)),
                  pl.BlockSpec((CMP_LEN, 2, 1, LANES), lambda b: (0, 0, 0, 0)),
                  pl.BlockSpec((2, LANES, LANES), lambda b: (0, 0, 0))],
        out_specs=pl.BlockSpec((1, ncp, 2 * LANES), lambda b: (b, 0, 0)),
        out_shape=jax.ShapeDtypeStruct((bsz, ncp, 2 * LANES), F32),
        scratch_shapes=[pltpu.VMEM((2, ncp + SUBLANES, LANES), F32)],
        compiler_params=_cparams(("parallel",)),
    )(proj, proj, w1bd, posrow, w2bd)


def _group_queries(q, g, nq):
    lane = lax.broadcasted_iota(jnp.int32, (nq, LANES), 1)
    in_g = (lane >= HEAD_DIM) if g else (lane < HEAD_DIM)
    return jnp.concatenate(
        [jnp.where(in_g, q[:, r * LANES:(r + 1) * LANES], 0.0) for r in range(R_NSA)], axis=0)


def _masked_softmax(s, mask):
    s = jnp.where(mask, s, NEG)
    m = jnp.max(s, axis=1, keepdims=True)
    p = jnp.where(mask, jnp.exp(s - m), 0.0)
    l = jnp.sum(p, axis=1, keepdims=True)
    return p / jnp.maximum(l, 1e-30)


def _select_blocks(imp, q_pos, n_blk, axis):
    nb = imp.shape[axis]
    blk = lax.broadcasted_iota(jnp.int32, imp.shape, axis)
    blk_f = blk.astype(F32)
    cur = q_pos // SEL_BLOCK
    forced = (blk == 0) | (blk == cur) | (blk == cur - 1)
    valid = blk * SEL_BLOCK <= q_pos
    score = jnp.where(forced, FORCE_SCORE, jnp.where(valid, imp, -1.0))
    score = jnp.where(blk < n_blk, score, -jnp.inf)
    sel = jnp.zeros(imp.shape, F32)
    for _ in range(SEL_TOPK):
        m = jnp.max(score, axis=axis, keepdims=True)
        first = jnp.min(jnp.where(score == m, blk_f, float(nb)), axis=axis, keepdims=True)
        pick = blk_f == first
        sel = jnp.where(pick, 1.0, sel)
        score = jnp.where(pick, -jnp.inf, score)
    return sel


def _gate_combine(sig, o_c, o_s, o_w, nq):
    lane = lax.broadcasted_iota(jnp.int32, (nq, LANES), 1)
    slabs = []
    for r in range(R_NSA):
        per_g = []
        for g in range(G_NSA):
            h = g * R_NSA + r
            rows = slice(r * nq, (r + 1) * nq)
            acc = sig[:, h:h + 1] * o_c[g][rows]
            acc = acc + sig[:, H_NSA + h:H_NSA + h + 1] * o_s[g][rows]
            acc = acc + sig[:, 2 * H_NSA + h:2 * H_NSA + h + 1] * o_w[g][rows]
            per_g.append(acc)
        slabs.append(jnp.where(lane < HEAD_DIM, per_g[0], per_g[1]))
    return jnp.concatenate(slabs, axis=1)


def _nsa_prompt_kernel(q_ref, gate_ref, kcvc_ref, selkv_ref, w0_ref, w1_ref, w2_ref, w3_ref, w4_ref,
                       tt_ref, tts_ref, tca_ref, tcb_ref, tcf_ref, mapt_ref, o_ref,
                       qg_ref, qa_ref, m_ref, acc_ref, *, ncp, n_blk):
    tq = TQ
    i = pl.program_id(1)
    q0 = i * tq
    rows = R_NSA * tq
    q = q_ref[0] * QK_SCALE
    lane = lax.broadcasted_iota(jnp.int32, (tq, LANES), 1)
    q_pos = q0 + lax.broadcasted_iota(jnp.int32, (tq, LANES), 0)
    nbp = mapt_ref.shape[0]
    q_pos_t = q0 + lax.broadcasted_iota(jnp.int32, (nbp, tq), 1)
    win_refs = (w0_ref, w1_ref, w2_ref, w3_ref, w4_ref)
    n_ctile = ncp // LANES
    jn_a = i // 16
    o_c, o_s, o_w = [], [], []

    for g in range(G_NSA):
        qg = _group_queries(q, g, tq)
        qg_ref[g] = _bf(qg)
        s = _dot3_t(qg, kcvc_ref[0, :, 0:LANES])
        n_idx = lax.broadcasted_iota(jnp.int32, (tq, ncp), 1)
        qp_c = q0 + lax.broadcasted_iota(jnp.int32, (tq, ncp), 0)
        mask_c = (n_idx * CMP_STRIDE + CMP_LEN - 1) <= qp_c
        bias_rows = []
        for r in range(R_NSA):
            h = g * R_NSA + r
            tiles = []
            for jn in range(n_ctile):
                t = jnp.where(jn == jn_a, tca_ref[0, h],
                              jnp.where(jn == jn_a - 1, tcb_ref[0, h], tcf_ref[0, h]))
                tiles.append(t)
            bias_rows.append(jnp.concatenate(tiles, axis=1) if n_ctile > 1 else tiles[0])
        bias_c = jnp.concatenate(bias_rows, axis=0)
        mask_c4 = jnp.concatenate([mask_c] * R_NSA, axis=0)
        p_c = _masked_softmax(s + bias_c, mask_c4)
        o_c.append(_dot(_bf(p_c), _bf(kcvc_ref[0, :, LANES:2 * LANES])))
        p_sum = p_c[0:tq]
        for r in range(1, R_NSA):
            p_sum = p_sum + p_c[r * tq:(r + 1) * tq]
        p_hi, p_lo = _split(p_sum)
        imp_t = _dot_t(mapt_ref[...], p_hi) + _dot_t(mapt_ref[...], p_lo)
        sel_t = _select_blocks(imp_t, q_pos_t, n_blk, axis=0)
        not_sel = _bf(1.0 - sel_t.T)
        qa_ref[g] = jnp.concatenate([_bf(qg * LOG2E), jnp.concatenate([not_sel] * R_NSA, axis=0)], axis=1)

    m_ref[...] = jnp.full((G_NSA, rows, LANES), NEG, F32)
    acc_ref[...] = jnp.zeros((G_NSA, rows, LANES), F32)
    causal4 = jnp.concatenate([(q0 + lane) <= q_pos] * R_NSA, axis=0)

    def sel_chunk(j0, nt, near, causal=False):
        width = nt * LANES
        k0 = pl.multiple_of(j0 * LANES, LANES)
        kv = selkv_ref[0, pl.ds(k0, width), :]
        v = kv[:, LANES:2 * LANES]
        v_lane = lax.broadcasted_iota(jnp.int32, (width, LANES), 1)
        key_blk = 2 * j0 + lax.broadcasted_iota(jnp.int32, (width, nbp), 0) // SEL_BLOCK
        blk_col = lax.broadcasted_iota(jnp.int32, (width, nbp), 1)
        k_aug = jnp.concatenate([_bf(kv[:, 0:LANES]),
                                 jnp.where(key_blk == blk_col, MASK_VALUE, 0.0).astype(BF16)], axis=1)
        scs = [_dot_t(qa_ref[g], k_aug) for g in range(G_NSA)]
        for g in range(G_NSA):
            vt = _bf(jnp.where((v_lane < HEAD_DIM) if g else (v_lane >= HEAD_DIM), 1.0, v))
            sc = scs[g]
            if near:
                bias_t = []
                for t in range(nt):
                    d = jnp.minimum(i - j0 - t, N_TOK_TILES - 1)
                    bias_t.append(jnp.concatenate([tts_ref[d, g * R_NSA + r] for r in range(R_NSA)], axis=0))
                sc = sc + (jnp.concatenate(bias_t, axis=1) if nt > 1 else bias_t[0])
            if causal:
                sc = jnp.where(causal4, sc, NEG)
            m_old = m_ref[g]
            m_new = jnp.maximum(m_old, jnp.max(sc, axis=1, keepdims=True))
            p = jnp.exp2(sc - (jnp.concatenate([m_new] * nt, axis=1) if nt > 1 else m_new))
            acc_ref[g] = jnp.exp2(m_old - m_new) * acc_ref[g] + _dot(_bf(p), vt)
            m_ref[g] = m_new

    sel_chunk(i, 1, True, causal=True)
    near_pairs = (N_TOK_TILES - 1) // 2

    def near_body(p, carry):
        sel_chunk(i - 2 - 2 * p, 2, True)
        return carry

    lax.fori_loop(0, jnp.minimum(near_pairs, i // 2), near_body, 0)

    def far_body(p, carry):
        sel_chunk(i - 2 - 2 * p, 2, False)
        return carry

    lax.fori_loop(near_pairs, i // 2, far_body, 0)

    @pl.when(i % 2 == 1)
    def _():
        sel_chunk(0, 1, True)

    for g in range(G_NSA):
        acc = acc_ref[g]
        den = acc[:, 0:1] if g else acc[:, LANES - 1:LANES]
        o_s.append(acc / den)

    for g in range(G_NSA):
        s_tiles, v_tiles, m_tiles = [], [], []
        for dlt in range(WINDOW // LANES + 1):
            kv = win_refs[dlt][0]
            kt = _bf(kv[:, 0:LANES])
            v_tiles.append(_bf(kv[:, LANES:2 * LANES]))
            dist = q_pos - ((i - dlt) * LANES + lane)
            mk = (dist >= 0) & (dist < WINDOW) & (i - dlt >= 0)
            bias = jnp.concatenate([tt_ref[dlt, g * R_NSA + r] for r in range(R_NSA)], axis=0)
            s_tiles.append(_dot_t(qg_ref[g], kt) + bias)
            m_tiles.append(jnp.concatenate([mk] * R_NSA, axis=0))
        p_w = _masked_softmax(jnp.concatenate(s_tiles, axis=1), jnp.concatenate(m_tiles, axis=1))
        o_w.append(_dot(_bf(p_w), jnp.concatenate(v_tiles, axis=0)))

    sig = jax.nn.sigmoid(gate_ref[0])
    o_ref[0] = _gate_combine(sig, o_c, o_s, o_w, tq)


def _nsa_prompt(proj, kcvc, tt, tts, tc, sel_map):
    bsz, t, _ = proj.shape
    tq = TQ
    ncp = t // CMP_STRIDE
    n_blk = t // SEL_BLOCK
    map_t = sel_map.T
    nbp = map_t.shape[0]
    rows = R_NSA * tq
    nwin = WINDOW // LANES + 1
    win_specs = [
        pl.BlockSpec((1, tq, 2 * LANES),
                     functools.partial(lambda b, i, d: (b, jnp.maximum(i - d, 0), (C_KVB + 4 * LANES) // (2 * LANES)), d=d))
        for d in range(nwin)]
    far = N_CMP_TILES - 1
    return pl.pallas_call(
        functools.partial(_nsa_prompt_kernel, ncp=ncp, n_blk=n_blk),
        grid=(bsz, t // tq),
        in_specs=[pl.BlockSpec((1, tq, W_NSA), lambda b, i: (b, i, C_QB // W_NSA)),
                  pl.BlockSpec((1, tq, LANES), lambda b, i: (b, i, C_GATE // LANES)),
                  pl.BlockSpec((1, ncp, 2 * LANES), lambda b, i: (b, 0, 0)),
                  pl.BlockSpec((1, t, 2 * LANES), lambda b, i: (b, 0, (C_KVB + 2 * LANES) // (2 * LANES)))]
                 + win_specs +
                 [pl.BlockSpec((nwin, H_NSA, tq, LANES), lambda b, i: (0, 0, 0, 0)),
                  pl.BlockSpec((N_TOK_TILES, H_NSA, tq, LANES), lambda b, i: (0, 0, 0, 0)),
                  pl.BlockSpec((1, H_NSA, tq, LANES), lambda b, i: (i % 16, 0, 0, 0)),
                  pl.BlockSpec((1, H_NSA, tq, LANES), lambda b, i: (jnp.minimum(i % 16 + 16, far), 0, 0, 0)),
                  pl.BlockSpec((1, H_NSA, tq, LANES), lambda b, i: (far, 0, 0, 0)),
                  pl.BlockSpec((nbp, ncp), lambda b, i: (0, 0))],
        out_specs=pl.BlockSpec((1, tq, W_NSA), lambda b, i: (b, i, 0)),
        out_shape=jax.ShapeDtypeStruct((bsz, t, W_NSA), F32),
        scratch_shapes=[pltpu.VMEM((G_NSA, rows, LANES), BF16),
                        pltpu.VMEM((G_NSA, rows, LANES + nbp), BF16),
                        pltpu.VMEM((G_NSA, rows, LANES), F32),
                        pltpu.VMEM((G_NSA, rows, LANES), F32)],
        compiler_params=_cparams(("parallel", "arbitrary")),
    )(proj, proj, kcvc, proj, *([proj] * nwin), tt[:nwin], tts, tc, tc, tc, map_t)


def _nsa_sample_kernel(pt_ref, p_ref, win_ref, cache_ref, w1_ref, pos_ref, w2_ref, map_ref,
                       bsel_ref, bcmp_ref, bwin_ref, o_ref, buf, sem, tail_ref, wbuf,
                       *, layer_base, n_pages, nq, n_blk):
    b = pl.program_id(0)
    past = n_pages * PAGE_SIZE
    tk = past + LANES
    ncp = past // CMP_STRIDE
    nbp = map_ref.shape[1]
    rows = R_NSA * nq
    wlen = win_ref.shape[1]
    wpad = wbuf.shape[0]

    n_slab = buf.shape[0]

    def page_copy(j, s):
        return pltpu.make_async_copy(
            cache_ref.at[layer_base + pt_ref[b, j], :, pl.ds(s * LANES, LANES)],
            buf.at[s, pl.ds(pl.multiple_of(j * PAGE_SIZE, PAGE_SIZE), PAGE_SIZE)], sem)

    def start(j, c):
        for s in range(n_slab):
            page_copy(j, s).start()
        return c

    def wait(j, c):
        for s in range(n_slab):
            page_copy(j, s).wait()
        return c

    lax.fori_loop(0, n_pages, start, 0)

    new_nsa = p_ref[0, :, C_KVB - C_QB:C_KVB - C_QB + 4 * LANES]
    for s in range(n_slab):
        buf[s, past:past + nq, :] = new_nsa[:, s * LANES:(s + 1) * LANES]
        buf[s, past + nq:tk, :] = jnp.zeros((LANES - nq, LANES), F32)
    wbuf[0:wlen, :] = win_ref[0]
    wbuf[wlen:wlen + nq, :] = p_ref[0, :, C_KVB - C_QB + 4 * LANES:C_KVB - C_QB + 6 * LANES]
    wbuf[wlen + nq:wpad, :] = jnp.zeros((wpad - wlen - nq, 2 * LANES), F32)

    q = p_ref[0, :, 0:W_NSA] * QK_SCALE
    sig = jax.nn.sigmoid(p_ref[0, :, C_GATE - C_QB:C_GATE - C_QB + LANES])
    q_pos_nb = past + lax.broadcasted_iota(jnp.int32, (nq, nbp), 0)

    lax.fori_loop(0, n_pages, wait, 0)

    kc, vc = _compress_rows((buf.at[0], buf.at[1]), ncp, w1_ref, pos_ref, w2_ref, tail_ref)
    n_idx = lax.broadcasted_iota(jnp.int32, (rows, ncp), 1)
    mask_c = n_idx < ncp - 1
    tok = lax.broadcasted_iota(jnp.int32, (rows, tk), 1)
    qi_tok = lax.broadcasted_iota(jnp.int32, (rows, tk), 0) % nq
    causal = tok <= past + qi_tok
    kk = lax.broadcasted_iota(jnp.int32, (rows, wpad), 1)
    dist_w = wlen + (lax.broadcasted_iota(jnp.int32, (rows, wpad), 0) % nq) - kk
    mask_w = (dist_w >= 0) & (dist_w < WINDOW) & (kk < wlen + nq)
    blk_row = lax.broadcasted_iota(jnp.int32, (nbp, tk), 0)
    blk_tok = lax.broadcasted_iota(jnp.int32, (nbp, tk), 1) // SEL_BLOCK
    expand = jnp.where(blk_row == blk_tok, 1.0, 0.0).astype(BF16)

    o_c, o_s, o_w = [], [], []
    for g in range(G_NSA):
        qg = _group_queries(q, g, nq)
        qg_bf = _bf(qg)
        brow = slice(g * rows, (g + 1) * rows)
        s = _dot3_t(qg, kc) + bcmp_ref[brow, :]
        p_c = _masked_softmax(s, mask_c)
        o_c.append(_dot(_bf(p_c), _bf(vc)))
        p_sum = p_c[0:nq]
        for r in range(1, R_NSA):
            p_sum = p_sum + p_c[r * nq:(r + 1) * nq]
        imp = _dot2_exact_rhs(p_sum, map_ref[...])
        sel = _select_blocks(imp, q_pos_nb, n_blk, axis=1)
        mk = _dot(_bf(jnp.concatenate([sel] * R_NSA, axis=0)), expand) > 0.5
        mk = jnp.logical_and(mk, causal)
        s = _dot_t(qg_bf, _bf(buf[2])) + bsel_ref[brow, :]
        p_s = _masked_softmax(s, mk)
        o_s.append(_dot(_bf(p_s), _bf(buf[3])))
        s = _dot_t(qg_bf, _bf(wbuf[:, 0:LANES])) + bwin_ref[brow, :]
        p_w = _masked_softmax(s, mask_w)
        o_w.append(_dot(_bf(p_w), _bf(wbuf[:, LANES:2 * LANES])))

    o_ref[0] = _gate_combine(sig, o_c, o_s, o_w, nq)


def _nsa_sample(proj_s, win_cache, cache_pages, page_table, w1bd, posrow, w2bd, sel_map,
                bsel, bcmp, bwin, layer, nq):
    dbsz, n_pages = page_table.shape
    n_pool = cache_pages.shape[0] // (win_cache.shape[0] // dbsz)
    past = n_pages * PAGE_SIZE
    tk = past + LANES
    ncp = past // CMP_STRIDE
    wlen = win_cache.shape[1]
    wpad = wlen + LANES
    n_blk = -(-(past + nq) // SEL_BLOCK)
    kern = functools.partial(_nsa_sample_kernel, layer_base=layer * n_pool, n_pages=n_pages,
                             nq=nq, n_blk=n_blk)
    const2 = lambda b, pt: (0, 0)
    grid_spec = pltpu.PrefetchScalarGridSpec(
        num_scalar_prefetch=1,
        grid=(dbsz,),
        in_specs=[pl.BlockSpec((1, nq, N_PROJ - C_QB), lambda b, pt: (0, b, 1)),
                  pl.BlockSpec((1, wlen, 2 * LANES), lambda b, pt: (layer * dbsz + b, 0, 0)),
                  pl.BlockSpec(memory_space=pl.ANY),
                  pl.BlockSpec((CMP_LEN, 2, LANES, LANES), lambda b, pt: (0, 0, 0, 0)),
                  pl.BlockSpec((CMP_LEN, 2, 1, LANES), lambda b, pt: (0, 0, 0, 0)),
                  pl.BlockSpec((2, LANES, LANES), lambda b, pt: (0, 0, 0)),
                  pl.BlockSpec(sel_map.shape, const2),
                  pl.BlockSpec(bsel.shape, const2),
                  pl.BlockSpec(bcmp.shape, const2),
                  pl.BlockSpec(bwin.shape, const2)],
        out_specs=pl.BlockSpec((1, nq, W_NSA), lambda b, pt: (0, b, 0)),
        scratch_shapes=[pltpu.VMEM((4, tk, LANES), F32),
                        pltpu.SemaphoreType.DMA(()),
                        pltpu.VMEM((2, ncp + SUBLANES, LANES), F32),
                        pltpu.VMEM((wpad, 2 * LANES), F32)],
    )
    return pl.pallas_call(
        kern, grid_spec=grid_spec,
        out_shape=jax.ShapeDtypeStruct((1, dbsz * nq, W_NSA), F32),
        compiler_params=_cparams(("arbitrary",)),
    )(page_table, proj_s, win_cache, cache_pages, w1bd, posrow, w2bd, sel_map, bsel, bcmp, bwin)


def _out_kernel(y_ref, oa_ref, ob_ref, za_ref, zb_ref, gate_ref, norm_ref, w_ref, g_ref, b_ref, o_ref,
                *, alpha):
    def gated_norm(o, z, nrm):
        ms = jnp.mean(o * o, axis=-1, keepdims=True)
        return o * lax.rsqrt(ms + RMS_EPS) * nrm * (z * jax.nn.sigmoid(z))

    ua = gated_norm(oa_ref[0], za_ref[0], norm_ref[:, 0:W_SB])
    ub = gated_norm(ob_ref[0], zb_ref[0], norm_ref[:, W_SB:D_MODEL])
    u = _bf(jnp.concatenate([ua, ub], axis=1))
    mixed = _dot(u, w_ref[...])
    x = alpha * y_ref[0] + gate_ref[0] * mixed
    mu = jnp.mean(x, axis=-1, keepdims=True)
    xc = x - mu
    var = jnp.mean(xc * xc, axis=-1, keepdims=True)
    o_ref[0] = xc * lax.rsqrt(var + LN_EPS) * g_ref[...] + b_ref[...]


def _out_stage(y, o_a, o_b, proj, gate, norm_p, w_out_bf, ln_g, ln_b, alpha):
    bsz, t, d = y.shape
    tm = min(256, t)
    if gate.shape[1] == 1:
        gate_spec = pl.BlockSpec((1, 1, d), lambda b, i: (b, 0, 0))
    else:
        gate_spec = pl.BlockSpec((1, tm, d), lambda b, i: (b, i, 0))
    row = lambda w: pl.BlockSpec((1, w), lambda b, i: (0, 0))
    return pl.pallas_call(
        functools.partial(_out_kernel, alpha=alpha),
        grid=(bsz, t // tm),
        in_specs=[pl.BlockSpec((1, tm, d), lambda b, i: (b, i, 0)),
                  pl.BlockSpec((1, tm, W_SB), lambda b, i: (b, i, 0)),
                  pl.BlockSpec((1, tm, W_NSA), lambda b, i: (b, i, 0)),
                  pl.BlockSpec((1, tm, W_SB), lambda b, i: (b, i, C_ZA // W_SB)),
                  pl.BlockSpec((1, tm, W_NSA), lambda b, i: (b, i, C_ZB // W_NSA)),
                  gate_spec, row(d),
                  pl.BlockSpec((d, d), lambda b, i: (0, 0)),
                  row(d), row(d)],
        out_specs=pl.BlockSpec((1, tm, d), lambda b, i: (b, i, 0)),
        out_shape=jax.ShapeDtypeStruct((bsz, t, d), F32),
        compiler_params=_cparams(("parallel", "parallel")),
    )(y, o_a, o_b, proj, proj, gate, norm_p.reshape(1, d), w_out_bf, ln_g.reshape(1, d), ln_b.reshape(1, d))


def _rel_bucket(dist):
    n = jnp.maximum(dist, 0)
    nf = jnp.maximum(n, 1).astype(F32)
    large = REL_MAX_EXACT + (jnp.log(nf / REL_MAX_EXACT) / math.log(REL_MAX_DIST / REL_MAX_EXACT)
                             * (N_BUCKETS - REL_MAX_EXACT)).astype(jnp.int32)
    large = jnp.minimum(large, N_BUCKETS - 1)
    return jnp.where(n < REL_MAX_EXACT, n, large)


def _nsa_perm(x, axis):
    shp = x.shape
    x = x.reshape(shp[:axis] + (G_NSA, R_NSA, HEAD_DIM) + shp[axis + 1:])
    x = jnp.swapaxes(x, axis, axis + 1)
    return x.reshape(shp)


def _bias_tile_kernel(base_ref, step_ref, thr_ref, rb_ref, o_ref):
    t = pl.program_id(0)
    shape = o_ref.shape[2:]
    dist = (base_ref[t] + lax.broadcasted_iota(jnp.int32, shape, 0)
            - step_ref[t] * lax.broadcasted_iota(jnp.int32, shape, 1))
    for h in range(H_NSA):
        val = jnp.full(shape, rb_ref[0, h], F32)
        for b in range(1, N_BUCKETS):
            val = jnp.where(dist >= thr_ref[b], rb_ref[b, h], val)
        o_ref[0, h] = val


def _bias_tiles(base, step, thr, rel_bias, rows):
    n = len(base)
    grid_spec = pltpu.PrefetchScalarGridSpec(
        num_scalar_prefetch=3,
        grid=(n,),
        in_specs=[pl.BlockSpec(memory_space=pltpu.SMEM)],
        out_specs=pl.BlockSpec((1, H_NSA, rows, LANES), lambda t, *_: (t, 0, 0, 0)),
    )
    return pl.pallas_call(
        _bias_tile_kernel, grid_spec=grid_spec,
        out_shape=jax.ShapeDtypeStruct((n, H_NSA, rows, LANES), F32),
        compiler_params=_cparams(("parallel",)),
    )(jnp.asarray(base, jnp.int32), jnp.asarray(step, jnp.int32), thr, rel_bias)


def _bias_tables(rel_bias, nq, past, wlen):
    bucket = _rel_bucket(jnp.arange(BIAS_DMAX))
    thr = jnp.sum(bucket[None, :] < jnp.arange(N_BUCKETS)[:, None], axis=1).astype(jnp.int32)
    base = [LANES * d for d in range(N_TOK_TILES)] + [LANES * o - (CMP_LEN - 1) for o in range(N_CMP_TILES)]
    step = [1] * N_TOK_TILES + [CMP_STRIDE] * N_CMP_TILES
    tiles = _bias_tiles(base, step, thr, rel_bias, TQ)
    tt, tc = tiles[:N_TOK_TILES], tiles[N_TOK_TILES:]
    n_key, n_cmp, n_win = past // LANES + 1, past // (CMP_STRIDE * LANES), wlen // LANES + 1
    base = ([past - LANES * k for k in range(n_key)]
            + [past - CMP_STRIDE * LANES * j - (CMP_LEN - 1) for j in range(n_cmp)]
            + [wlen - LANES * k for k in range(n_win)])
    step = [1] * n_key + [CMP_STRIDE] * n_cmp + [1] * n_win
    tiles = _bias_tiles(base, step, thr, rel_bias, nq)
    rows = lambda x: jnp.transpose(x, (1, 2, 0, 3)).reshape(H_NSA * nq, x.shape[0] * LANES)
    bsel, bcmp, bwin = rows(tiles[:n_key]), rows(tiles[n_key:n_key + n_cmp]), rows(tiles[n_key + n_cmp:])
    return tt, tc, bsel, bcmp, bwin


def _selection_map(ncp, n_c, n_blk):
    nbp = -(-n_blk // LANES) * LANES
    cs = np.arange(ncp)[:, None] * CMP_STRIDE
    ss = np.arange(nbp)[None, :] * SEL_BLOCK
    ov = np.minimum(cs + CMP_LEN, ss + SEL_BLOCK) - np.maximum(cs, ss)
    m = np.clip(ov, 0, None).astype(np.float32) / CMP_LEN
    m = m * (np.arange(ncp)[:, None] < n_c) * (np.arange(nbp)[None, :] < n_blk)
    return jnp.asarray(m, dtype=BF16)


def _block_diag2(w):
    z = jnp.zeros_like(w)
    return jnp.concatenate([jnp.concatenate([w, z], axis=-1), jnp.concatenate([z, w], axis=-1)], axis=-2)


def kernel(x_prompt, x_sample, cache_sb_kv, cache_nsa_kv, cache_win_kv, page_table, c_prompt, c_sample,
           ln_in_g, ln_in_b, w_ada, b_ada, w_in, w_cmp1, w_cmp2, pos_cmp, norm_grp, w_out, ln_g, ln_b,
           rel_bias):
    bsz, seq, d = x_prompt.shape
    dbsz, nq, _ = x_sample.shape
    depth = w_ada.shape[0]
    n_pool = cache_sb_kv.shape[1]
    n_pages = page_table.shape[1]
    past = n_pages * PAGE_SIZE
    wlen = cache_win_kv.shape[2]
    alpha = (2 * depth) ** 0.25
    assert d == D_MODEL and seq % (16 * TQ) == 0 and past % (16 * TQ) == 0 and wlen == WINDOW
    assert seq // SEL_BLOCK >= SEL_TOPK and nq == SUBLANES

    w_qb = _nsa_perm(w_in[:, :, 2048:2560], 2)
    w_zb = _nsa_perm(w_in[:, :, 3352:3864], 2)
    pad = lambda n: jnp.zeros((depth, d, n), F32)
    w_in_p = _bf(jnp.concatenate(
        [w_in[:, :, 0:2048], w_qb, w_in[:, :, 2560:3328], w_in[:, :, 3328:3352], pad(C_ZB - C_GATE - 3 * H_NSA),
         w_zb], axis=2))
    norm_p = jnp.concatenate([norm_grp[:, :W_SB], _nsa_perm(norm_grp[:, W_SB:], 1)], axis=1)
    w_out_p = _bf(jnp.concatenate([w_out[:, :W_SB], _nsa_perm(w_out[:, W_SB:], 1)], axis=1))
    w1bd = _bf(jnp.swapaxes(_block_diag2(w_cmp1), 1, 2))
    w2bd = _bf(_block_diag2(w_cmp2))
    posrow = jnp.swapaxes(jnp.concatenate([pos_cmp, pos_cmp], axis=-1), 1, 2)[:, :, :, None, :]
    tt, tc, bsel, bcmp, bwin = _bias_tables(rel_bias, nq, past, wlen)
    tts = (tt - tt[N_TOK_TILES - 1:]) * LOG2E
    map_p = _selection_map(seq // CMP_STRIDE, (seq - CMP_LEN) // CMP_STRIDE + 1, seq // SEL_BLOCK)
    n_blk_s = -(-(past + nq) // SEL_BLOCK)
    map_s = _selection_map(past // CMP_STRIDE, (past + nq - CMP_LEN) // CMP_STRIDE + 1, n_blk_s)

    n_c = bsz + dbsz
    c_rows = -(-n_c // SUBLANES) * SUBLANES
    c_all = jnp.concatenate([c_prompt, c_sample, jnp.zeros((c_rows - n_c, d), F32)], axis=0)
    mod = _ada_mod(c_all, w_ada, b_ada)

    y_p = _layer_norm(x_prompt.reshape(bsz * seq, d), ln_in_g, ln_in_b).reshape(bsz, seq, d)
    y_s = _layer_norm(x_sample.reshape(dbsz * nq, d), ln_in_g, ln_in_b).reshape(1, dbsz * nq, d)

    sb_pages = jnp.transpose(cache_sb_kv, (0, 1, 3, 4, 5, 2)).reshape(depth * n_pool, 2, H_SB, HEAD_DIM, PAGE_SIZE)
    nsa_pages = cache_nsa_kv.reshape(depth * n_pool, PAGE_SIZE, 4 * KV_NSA)
    win_cache = cache_win_kv.reshape(depth * dbsz, wlen, 2 * KV_NSA)

    p_sb, p_nsa, p_win, s_sb, s_nsa, s_win = [], [], [], [], [], []
    for l in range(depth):
        mp = mod[l, :bsz].reshape(bsz, 1, 3 * d)
        proj = _in_proj(y_p, mp[:, :, 0:d], mp[:, :, d:2 * d], w_in_p[l])
        o_a = _sb_prompt(proj)
        kcvc = _compress_prompt(proj, w1bd[l], posrow[l], w2bd[l])
        o_b = _nsa_prompt(proj, kcvc, tt, tts, tc, map_p)
        p_sb.append(proj[:, :, C_KA:C_ZA].reshape(bsz, seq, 2, H_SB, HEAD_DIM))
        p_nsa.append(proj[:, :, C_KVB:C_KVB + 4 * KV_NSA].reshape(bsz, seq, 4, G_NSA, HEAD_DIM))
        p_win.append(proj[:, seq - min(WINDOW, seq):, C_KVB + 4 * KV_NSA:C_KVB + 6 * KV_NSA]
                     .reshape(bsz, min(WINDOW, seq), 2, G_NSA, HEAD_DIM))
        y_p = _out_stage(y_p, o_a, o_b, proj, mp[:, :, 2 * d:3 * d], norm_p[l], w_out_p[l], ln_g[l], ln_b[l],
                         alpha)
        ms = jnp.repeat(mod[l, bsz:bsz + dbsz], nq, axis=0).reshape(1, dbsz * nq, 3 * d)
        proj_s = _in_proj(y_s, ms[:, :, 0:d], ms[:, :, d:2 * d], w_in_p[l])
        o_a = _sb_sample(proj_s, sb_pages, page_table, l * n_pool, nq)
        o_b = _nsa_sample(proj_s, win_cache, nsa_pages, page_table, w1bd[l], posrow[l], w2bd[l], map_s,
                          bsel, bcmp, bwin, l, nq)
        ps = proj_s.reshape(dbsz, nq, N_PROJ)
        s_sb.append(ps[:, :, C_KA:C_ZA].reshape(dbsz, nq, 2, H_SB, HEAD_DIM))
        s_nsa.append(ps[:, :, C_KVB:C_KVB + 4 * KV_NSA].reshape(dbsz, nq, 4, G_NSA, HEAD_DIM))
        new_win = ps[:, :, C_KVB + 4 * KV_NSA:C_KVB + 6 * KV_NSA].reshape(dbsz, nq, 2, G_NSA, HEAD_DIM)
        win_all = jnp.concatenate([cache_win_kv[l], new_win], axis=1)
        s_win.append(win_all[:, win_all.shape[1] - min(WINDOW, past + nq):])
        y_s = _out_stage(y_s, o_a, o_b, proj_s, ms[:, :, 2 * d:3 * d], norm_p[l], w_out_p[l], ln_g[l],
                         ln_b[l], alpha)

    return (y_p, y_s.reshape(dbsz, nq, d), jnp.stack(p_sb), jnp.stack(p_nsa), jnp.stack(p_win),
            jnp.stack(s_sb), jnp.stack(s_nsa), jnp.stack(s_win))
```

```python
import functools
import math

import numpy as np
import jax
import jax.numpy as jnp
from jax import lax
from jax.experimental import pallas as pl
from jax.experimental.pallas import tpu as pltpu

F32 = jnp.float32
BF16 = jnp.bfloat16

D_MODEL = 1024
HEAD_DIM = 64
W_SB = D_MODEL // 2
W_NSA = D_MODEL - W_SB
H_SB = W_SB // HEAD_DIM
H_NSA = W_NSA // HEAD_DIM
G_NSA = 2
R_NSA = H_NSA // G_NSA
KV_NSA = G_NSA * HEAD_DIM
CMP_LEN = 32
CMP_STRIDE = 16
SEL_BLOCK = 64
SEL_TOPK = 16
WINDOW = 512
PAGE_SIZE = 128
N_BUCKETS = 32
REL_MAX_EXACT = N_BUCKETS // 2
REL_MAX_DIST = 1024
LN_EPS = 1e-5
RMS_EPS = 1e-6
NEG = -1e30
MASK_VALUE = -1e30
FORCE_SCORE = 1e4
QK_SCALE = HEAD_DIM ** -0.5
LOG2E = math.log2(math.e)

LANES = 128
SUBLANES = 8
VMEM_LIMIT = 56 * 1024 * 1024

C_QA, C_KA, C_VA, C_ZA, C_QB, C_KVB = 0, 512, 1024, 1536, 2048, 2560
C_GATE, C_ZB, N_PROJ = 3328, 3584, 4096
SB_DEAD = -104.0
BIAS_DMAX = 1024
N_TOK_TILES = 9
N_CMP_TILES = 24
TQ = 128


def _bf(x):
    return x.astype(BF16)


def _dot(a, b):
    return jnp.dot(a, b, preferred_element_type=F32)


def _dot_t(a, b):
    return lax.dot_general(a, b, (((1,), (1,)), ((), ())), preferred_element_type=F32)


def _split(x):
    hi = _bf(x)
    lo = _bf(x - hi.astype(F32))
    return hi, lo


def _dot3_t(a, b):
    ah, al = _split(a)
    bh, bl = _split(b)
    return _dot_t(ah, bh) + _dot_t(ah, bl) + _dot_t(al, bh)


def _dot3(a, b):
    ah, al = _split(a)
    bh, bl = _split(b)
    return _dot(ah, bh) + _dot(ah, bl) + _dot(al, bh)


def _dot2_exact_rhs(a, b_bf):
    ah, al = _split(a)
    return _dot(ah, b_bf) + _dot(al, b_bf)


def _softplus(z):
    return jnp.maximum(z, 0.0) + jnp.log1p(jnp.exp(-jnp.abs(z)))


def _cparams(sem):
    return pltpu.CompilerParams(dimension_semantics=sem, vmem_limit_bytes=VMEM_LIMIT)


def _ln_kernel(x_ref, g_ref, b_ref, o_ref):
    x = x_ref[...]
    mu = jnp.mean(x, axis=-1, keepdims=True)
    xc = x - mu
    var = jnp.mean(xc * xc, axis=-1, keepdims=True)
    o_ref[...] = xc * lax.rsqrt(var + LN_EPS) * g_ref[...] + b_ref[...]


def _layer_norm(x2d, g, b):
    rows, d = x2d.shape
    tm = min(512, rows)
    return pl.pallas_call(
        _ln_kernel,
        grid=(rows // tm,),
        in_specs=[pl.BlockSpec((tm, d), lambda i: (i, 0)),
                  pl.BlockSpec((1, d), lambda i: (0, 0)),
                  pl.BlockSpec((1, d), lambda i: (0, 0))],
        out_specs=pl.BlockSpec((tm, d), lambda i: (i, 0)),
        out_shape=jax.ShapeDtypeStruct((rows, d), F32),
        compiler_params=_cparams(("parallel",)),
    )(x2d, g.reshape(1, d), b.reshape(1, d))


def _mod_kernel(c_ref, w_ref, b_ref, o_ref):
    c = c_ref[...]
    s = c * jax.nn.sigmoid(c)
    o_ref[0] = _dot3(s, w_ref[0]) + b_ref[0]


def _ada_mod(c_all, w_ada, b_ada):
    depth, d, n3 = w_ada.shape
    rows = c_all.shape[0]
    tn = 1024
    return pl.pallas_call(
        _mod_kernel,
        grid=(depth, n3 // tn),
        in_specs=[pl.BlockSpec((rows, d), lambda l, j: (0, 0)),
                  pl.BlockSpec((1, d, tn), lambda l, j: (l, 0, j)),
                  pl.BlockSpec((1, 1, tn), lambda l, j: (l, 0, j))],
        out_specs=pl.BlockSpec((1, rows, tn), lambda l, j: (l, 0, j)),
        out_shape=jax.ShapeDtypeStruct((depth, rows, n3), F32),
        compiler_params=_cparams(("parallel", "parallel")),
    )(c_all, w_ada, b_ada.reshape(depth, 1, n3))


def _inproj_kernel(y_ref, sh_ref, sc_ref, w_ref, o_ref):
    h = y_ref[0] * (1.0 + sc_ref[0]) + sh_ref[0]
    hb = _bf(h)
    chunk = 512
    for c0 in range(0, N_PROJ, chunk):
        o_ref[0, :, c0:c0 + chunk] = _dot(hb, w_ref[:, c0:c0 + chunk])


def _in_proj(y, shift, scale, w_bf):
    bsz, t, d = y.shape
    tm = min(256, t)
    tmod = shift.shape[1]
    if tmod == 1:
        mod_spec = pl.BlockSpec((1, 1, d), lambda b, i: (b, 0, 0))
    else:
        mod_spec = pl.BlockSpec((1, tm, d), lambda b, i: (b, i, 0))
    return pl.pallas_call(
        _inproj_kernel,
        grid=(bsz, t // tm),
        in_specs=[pl.BlockSpec((1, tm, d), lambda b, i: (b, i, 0)), mod_spec, mod_spec,
                  pl.BlockSpec((d, N_PROJ), lambda b, i: (0, 0))],
        out_specs=pl.BlockSpec((1, tm, N_PROJ), lambda b, i: (b, i, 0)),
        out_shape=jax.ShapeDtypeStruct((bsz, t, N_PROJ), F32),
        compiler_params=_cparams(("parallel", "parallel")),
    )(y, shift, scale, w_bf)


def _sb_tile(qh, kt, vt, mask, carry, tri):
    a, carry = _sb_weights(_dot_t(qh, kt), mask, carry, tri)
    return _dot(_bf(a), vt), carry


def _sb_weights(z, mask, carry, tri):
    lk = -_softplus(z)
    if mask is not None:
        lk = jnp.where(mask, lk, 0.0)
    after = []
    for blk in reversed(range(z.shape[1] // LANES)):
        lk_b = lk[:, blk * LANES:(blk + 1) * LANES]
        hi, lo = _split(lk_b)
        after.insert(0, _dot(hi, tri) + _dot(lo, tri) + carry)
        carry = carry + jnp.sum(lk_b, axis=1, keepdims=True)
    after = jnp.concatenate(after, axis=1) if len(after) > 1 else after[0]
    a = jnp.exp(z + lk + after)
    if mask is not None:
        a = jnp.where(mask, a, 0.0)
    return a, carry


def _tri_newer():
    r = lax.broadcasted_iota(jnp.int32, (LANES, LANES), 0)
    c = lax.broadcasted_iota(jnp.int32, (LANES, LANES), 1)
    return jnp.where(r > c, 1.0, 0.0).astype(BF16)


def _sb_prompt_kernel(q_ref, k_ref, v_ref, o_ref, *, tq):
    i = pl.program_id(2)
    q0 = i * tq
    q = q_ref[0] * QK_SCALE
    lane = lax.broadcasted_iota(jnp.int32, (tq, LANES), 1)
    row_pos = q0 + lax.broadcasted_iota(jnp.int32, (tq, LANES), 0)
    tri = _tri_newer()
    qh = _bf(jnp.concatenate([jnp.where(lane < HEAD_DIM, q, 0.0), jnp.where(lane >= HEAD_DIM, q, 0.0)], axis=0))
    def chunk(c, mask, carry):
        k0 = pl.multiple_of(c * tq, tq)
        kt = _bf(k_ref[0, pl.ds(k0, tq), :])
        vt = _bf(v_ref[0, pl.ds(k0, tq), :])
        return _sb_tile(qh, kt, vt, mask, carry, tri)

    key_pos = q0 + lax.broadcasted_iota(jnp.int32, (2 * tq, tq), 1)
    qry_pos = q0 + lax.broadcasted_iota(jnp.int32, (2 * tq, tq), 0) % tq
    acc, carry = chunk(i, key_pos < qry_pos, jnp.zeros((2 * tq, LANES), F32))

    def cond(s):
        c, alive, _, _ = s
        return jnp.logical_and(c >= 0, alive)

    def body(s):
        c, _, carry, acc = s
        contrib, carry = chunk(c, None, carry)
        return c - 1, jnp.max(carry) > SB_DEAD, carry, acc + contrib

    _, _, _, acc = lax.while_loop(cond, body, (i - 1, jnp.max(carry) > SB_DEAD, carry, acc))
    o_ref[0] = jnp.where(lane < HEAD_DIM, acc[0:tq], acc[tq:2 * tq])


def _sb_prompt(proj, tq=256):
    bsz, t, _ = proj.shape
    tq = min(tq, t)
    npair = W_SB // LANES
    return pl.pallas_call(
        functools.partial(_sb_prompt_kernel, tq=tq),
        grid=(bsz, npair, t // tq),
        in_specs=[pl.BlockSpec((1, tq, LANES), lambda b, p, i: (b, i, C_QA // LANES + p)),
                  pl.BlockSpec((1, t, LANES), lambda b, p, i: (b, 0, C_KA // LANES + p)),
                  pl.BlockSpec((1, t, LANES), lambda b, p, i: (b, 0, C_VA // LANES + p))],
        out_specs=pl.BlockSpec((1, tq, LANES), lambda b, p, i: (b, i, p)),
        out_shape=jax.ShapeDtypeStruct((bsz, t, W_SB), F32),
        compiler_params=_cparams(("parallel", "parallel", "arbitrary")),
    )(proj, proj, proj)


def _sb_sample_kernel(pt_ref, p_ref, last_ref, cache_ref, o_ref, buf, sem, carry_ref, acc_ref,
                      *, layer_base, n_pages, nq):
    b = pl.program_id(0)
    rows = H_SB * nq
    q = p_ref[0, :, C_QA:C_QA + W_SB] * QK_SCALE
    lane_c = lax.broadcasted_iota(jnp.int32, (rows, W_SB), 1)
    row_c = lax.broadcasted_iota(jnp.int32, (rows, W_SB), 0)
    own = (lane_c // HEAD_DIM) == (row_c // nq)
    qbd = _bf(jnp.where(own, jnp.concatenate([q] * H_SB, axis=0), 0.0))
    tri = _tri_newer()
    key = lax.broadcasted_iota(jnp.int32, (rows, LANES), 1)
    qi = lax.broadcasted_iota(jnp.int32, (rows, LANES), 0) % nq

    pad = jnp.zeros((LANES - nq, W_SB), F32)
    k_new = _bf(jnp.concatenate([p_ref[0, :, C_KA:C_KA + W_SB], pad], axis=0))
    v_new = _bf(jnp.concatenate([p_ref[0, :, C_VA:C_VA + W_SB], pad], axis=0))
    contrib, carry = _sb_tile(qbd, k_new, v_new, key < qi, jnp.zeros((rows, LANES), F32), tri)
    carry_ref[...] = carry
    acc_ref[...] = contrib

    def page_tile(kv_ref):
        kt_t = _bf(kv_ref[0].reshape(W_SB, PAGE_SIZE))
        vt_t = _bf(kv_ref[1].reshape(W_SB, PAGE_SIZE))
        a, carry = _sb_weights(_dot(qbd, kt_t), None, carry_ref[...], tri)
        acc_ref[...] += _dot_t(_bf(a), vt_t)
        carry_ref[...] = carry

    @pl.when(jnp.max(carry_ref[...]) > SB_DEAD)
    def _():
        page_tile(last_ref.at[0])

    def cond(c):
        j, alive = c
        return jnp.logical_and(j >= 0, alive)

    def body(c):
        j, _ = c
        page = layer_base + pt_ref[b, j]
        cp = pltpu.make_async_copy(cache_ref.at[page], buf, sem)
        cp.start()
        cp.wait()
        page_tile(buf)
        return j - 1, jnp.max(carry_ref[...]) > SB_DEAD

    lax.while_loop(cond, body, (n_pages - 2, jnp.max(carry_ref[...]) > SB_DEAD))

    acc = jnp.where(own, acc_ref[...], 0.0)
    o = acc[0:nq]
    for h in range(1, H_SB):
        o = o + acc[h * nq:(h + 1) * nq]
    o_ref[0] = o


def _sb_sample(proj_s, cache_pages, page_table, layer_base, nq):
    dbsz, n_pages = page_table.shape
    rows = H_SB * nq
    page_shape = cache_pages.shape[1:]
    kern = functools.partial(_sb_sample_kernel, layer_base=layer_base, n_pages=n_pages, nq=nq)
    grid_spec = pltpu.PrefetchScalarGridSpec(
        num_scalar_prefetch=1,
        grid=(dbsz,),
        in_specs=[pl.BlockSpec((1, nq, C_QB), lambda b, pt: (0, b, 0)),
                  pl.BlockSpec((1,) + page_shape,
                               lambda b, pt: (layer_base + pt[b, n_pages - 1], 0, 0, 0, 0)),
                  pl.BlockSpec(memory_space=pl.ANY)],
        out_specs=pl.BlockSpec((1, nq, W_SB), lambda b, pt: (0, b, 0)),
        scratch_shapes=[pltpu.VMEM(page_shape, F32),
                        pltpu.SemaphoreType.DMA(()),
                        pltpu.VMEM((rows, LANES), F32),
                        pltpu.VMEM((rows, W_SB), F32)],
    )
    return pl.pallas_call(
        kern, grid_spec=grid_spec,
        out_shape=jax.ShapeDtypeStruct((1, dbsz * nq, W_SB), F32),
        compiler_params=_cparams(("arbitrary",)),
    )(page_table, proj_s, cache_pages, cache_pages)


def _compress_rows(x_refs, ncp, w1_ref, pos_ref, w2_ref, tail_ref):
    outs = []
    for c, x_ref in enumerate(x_refs):
        head = jnp.zeros((ncp, LANES), F32)
        tail = jnp.zeros((ncp, LANES), F32)
        for l in range(CMP_STRIDE):
            r = x_ref[pl.ds(l, ncp, stride=CMP_STRIDE), :]
            head = head + _dot(_bf(r + pos_ref[l, c]), w1_ref[l, c])
            tail = tail + _dot(_bf(r + pos_ref[CMP_STRIDE + l, c]), w1_ref[CMP_STRIDE + l, c])
        tail_ref[c, 0:ncp, :] = tail
        tail_ref[c, ncp:ncp + SUBLANES, :] = jnp.zeros((SUBLANES, LANES), F32)
        hid = jax.nn.gelu(head + tail_ref[c, pl.ds(1, ncp), :])
        outs.append(_dot(_bf(hid), w2_ref[c]))
    return outs


def _compress_kernel(xk_ref, xv_ref, w1_ref, pos_ref, w2_ref, o_ref, tail_ref, *, ncp):
    kc, vc = _compress_rows((xk_ref.at[0], xv_ref.at[0]), ncp, w1_ref, pos_ref, w2_ref, tail_ref)
    o_ref[0, :, 0:LANES] = kc
    o_ref[0, :, LANES:2 * LANES] = vc


def _compress_prompt(proj, w1bd, posrow, w2bd):
    bsz, t, _ = proj.shape
    ncp = t // CMP_STRIDE
    return pl.pallas_call(
        functools.partial(_compress_kernel, ncp=ncp),
        grid=(bsz,),
        in_specs=[pl.BlockSpec((1, t, LANES), lambda b: (b, 0, C_KVB // LANES)),
                  pl.BlockSpec((1, t, LANES), lambda b: (b, 0, C_KVB // LANES + 1)),
                  pl.BlockSpec((CMP_LEN, 2, LANES, LANES), lambda b: (0, 0, 0, 0)),
                  pl.BlockSpec((CMP_LEN, 2, 1, LANES), lambda b: (0, 0, 0, 0)),
                  pl.BlockSpec((2, LANES, LANES), lambda b: (0, 0, 0))],
        out_specs=pl.BlockSpec((1, ncp, 2 * LANES), lambda b: (b, 0, 0)),
        out_shape=jax.ShapeDtypeStruct((bsz, ncp, 2 * LANES), F32),
        scratch_shapes=[pltpu.VMEM((2, ncp + SUBLANES, LANES), F32)],
        compiler_params=_cparams(("parallel",)),
    )(proj, proj, w1bd, posrow, w2bd)


def _group_queries(q, g, nq):
    lane = lax.broadcasted_iota(jnp.int32, (nq, LANES), 1)
    in_g = (lane >= HEAD_DIM) if g else (lane < HEAD_DIM)
    return jnp.concatenate(
        [jnp.where(in_g, q[:, r * LANES:(r + 1) * LANES], 0.0) for r in range(R_NSA)], axis=0)


def _masked_softmax(s, mask):
    s = jnp.where(mask, s, NEG)
    m = jnp.max(s, axis=1, keepdims=True)
    p = jnp.where(mask, jnp.exp(s - m), 0.0)
    l = jnp.sum(p, axis=1, keepdims=True)
    return p / jnp.maximum(l, 1e-30)


def _select_blocks(imp, q_pos, n_blk, axis):
    nb = imp.shape[axis]
    blk = lax.broadcasted_iota(jnp.int32, imp.shape, axis)
    blk_f = blk.astype(F32)
    cur = q_pos // SEL_BLOCK
    forced = (blk == 0) | (blk == cur) | (blk == cur - 1)
    valid = blk * SEL_BLOCK <= q_pos
    score = jnp.where(forced, FORCE_SCORE, jnp.where(valid, imp, -1.0))
    score = jnp.where(blk < n_blk, score, -jnp.inf)
    sel = jnp.zeros(imp.shape, F32)
    for _ in range(SEL_TOPK):
        m = jnp.max(score, axis=axis, keepdims=True)
        first = jnp.min(jnp.where(score == m, blk_f, float(nb)), axis=axis, keepdims=True)
        pick = blk_f == first
        sel = jnp.where(pick, 1.0, sel)
        score = jnp.where(pick, -jnp.inf, score)
    return sel


def _gate_combine(sig, o_c, o_s, o_w, nq):
    lane = lax.broadcasted_iota(jnp.int32, (nq, LANES), 1)
    slabs = []
    for r in range(R_NSA):
        per_g = []
        for g in range(G_NSA):
            h = g * R_NSA + r
            rows = slice(r * nq, (r + 1) * nq)
            acc = sig[:, h:h + 1] * o_c[g][rows]
            acc = acc + sig[:, H_NSA + h:H_NSA + h + 1] * o_s[g][rows]
            acc = acc + sig[:, 2 * H_NSA + h:2 * H_NSA + h + 1] * o_w[g][rows]
            per_g.append(acc)
        slabs.append(jnp.where(lane < HEAD_DIM, per_g[0], per_g[1]))
    return jnp.concatenate(slabs, axis=1)


def _nsa_prompt_kernel(q_ref, gate_ref, kcvc_ref, selkv_ref, w0_ref, w1_ref, w2_ref, w3_ref, w4_ref,
                       tt_ref, tts_ref, tca_ref, tcb_ref, tcf_ref, mapt_ref, o_ref,
                       qg_ref, qa_ref, m_ref, acc_ref, *, ncp, n_blk):
    tq = TQ
    i = pl.program_id(1)
    q0 = i * tq
    rows = R_NSA * tq
    q = q_ref[0] * QK_SCALE
    lane = lax.broadcasted_iota(jnp.int32, (tq, LANES), 1)
    q_pos = q0 + lax.broadcasted_iota(jnp.int32, (tq, LANES), 0)
    nbp = mapt_ref.shape[0]
    q_pos_t = q0 + lax.broadcasted_iota(jnp.int32, (nbp, tq), 1)
    win_refs = (w0_ref, w1_ref, w2_ref, w3_ref, w4_ref)
    n_ctile = ncp // LANES
    jn_a = i // 16
    o_c, o_s, o_w = [], [], []

    for g in range(G_NSA):
        qg = _group_queries(q, g, tq)
        qg_ref[g] = _bf(qg)
        s = _dot3_t(qg, kcvc_ref[0, :, 0:LANES])
        n_idx = lax.broadcasted_iota(jnp.int32, (tq, ncp), 1)
        qp_c = q0 + lax.broadcasted_iota(jnp.int32, (tq, ncp), 0)
        mask_c = (n_idx * CMP_STRIDE + CMP_LEN - 1) <= qp_c
        bias_rows = []
        for r in range(R_NSA):
            h = g * R_NSA + r
            tiles = []
            for jn in range(n_ctile):
                t = jnp.where(jn == jn_a, tca_ref[0, h],
                              jnp.where(jn == jn_a - 1, tcb_ref[0, h], tcf_ref[0, h]))
                tiles.append(t)
            bias_rows.append(jnp.concatenate(tiles, axis=1) if n_ctile > 1 else tiles[0])
        bias_c = jnp.concatenate(bias_rows, axis=0)
        mask_c4 = jnp.concatenate([mask_c] * R_NSA, axis=0)
        p_c = _masked_softmax(s + bias_c, mask_c4)
        o_c.append(_dot(_bf(p_c), _bf(kcvc_ref[0, :, LANES:2 * LANES])))
        p_sum = p_c[0:tq]
        for r in range(1, R_NSA):
            p_sum = p_sum + p_c[r * tq:(r + 1) * tq]
        p_hi, p_lo = _split(p_sum)
        imp_t = _dot_t(mapt_ref[...], p_hi) + _dot_t(mapt_ref[...], p_lo)
        sel_t = _select_blocks(imp_t, q_pos_t, n_blk, axis=0)
        not_sel = _bf(1.0 - sel_t.T)
        qa_ref[g] = jnp.concatenate([_bf(qg * LOG2E), jnp.concatenate([not_sel] * R_NSA, axis=0)], axis=1)

    m_ref[...] = jnp.full((G_NSA, rows, LANES), NEG, F32)
    acc_ref[...] = jnp.zeros((G_NSA, rows, LANES), F32)
    causal4 = jnp.concatenate([(q0 + lane) <= q_pos] * R_NSA, axis=0)

    def sel_chunk(j0, nt, near, causal=False):
        width = nt * LANES
        k0 = pl.multiple_of(j0 * LANES, LANES)
        kv = selkv_ref[0, pl.ds(k0, width), :]
        v = kv[:, LANES:2 * LANES]
        v_lane = lax.broadcasted_iota(jnp.int32, (width, LANES), 1)
        key_blk = 2 * j0 + lax.broadcasted_iota(jnp.int32, (width, nbp), 0) // SEL_BLOCK
        blk_col = lax.broadcasted_iota(jnp.int32, (width, nbp), 1)
        k_aug = jnp.concatenate([_bf(kv[:, 0:LANES]),
                                 jnp.where(key_blk == blk_col, MASK_VALUE, 0.0).astype(BF16)], axis=1)
        scs = [_dot_t(qa_ref[g], k_aug) for g in range(G_NSA)]
        for g in range(G_NSA):
            vt = _bf(jnp.where((v_lane < HEAD_DIM) if g else (v_lane >= HEAD_DIM), 1.0, v))
            sc = scs[g]
            if near:
                bias_t = []
                for t in range(nt):
                    d = jnp.minimum(i - j0 - t, N_TOK_TILES - 1)
                    bias_t.append(jnp.concatenate([tts_ref[d, g * R_NSA + r] for r in range(R_NSA)], axis=0))
                sc = sc + (jnp.concatenate(bias_t, axis=1) if nt > 1 else bias_t[0])
            if causal:
                sc = jnp.where(causal4, sc, NEG)
            m_old = m_ref[g]
            m_new = jnp.maximum(m_old, jnp.max(sc, axis=1, keepdims=True))
            p = jnp.exp2(sc - (jnp.concatenate([m_new] * nt, axis=1) if nt > 1 else m_new))
            acc_ref[g] = jnp.exp2(m_old - m_new) * acc_ref[g] + _dot(_bf(p), vt)
            m_ref[g] = m_new

    sel_chunk(i, 1, True, causal=True)
    near_pairs = (N_TOK_TILES - 1) // 2

    def near_body(p, carry):
        sel_chunk(i - 2 - 2 * p, 2, True)
        return carry

    lax.fori_loop(0, jnp.minimum(near_pairs, i // 2), near_body, 0)

    def far_body(p, carry):
        sel_chunk(i - 2 - 2 * p, 2, False)
        return carry

    lax.fori_loop(near_pairs, i // 2, far_body, 0)

    @pl.when(i % 2 == 1)
    def _():
        sel_chunk(0, 1, True)

    for g in range(G_NSA):
        acc = acc_ref[g]
        den = acc[:, 0:1] if g else acc[:, LANES - 1:LANES]
        o_s.append(acc / den)

    for g in range(G_NSA):
        s_tiles, v_tiles, m_tiles = [], [], []
        for dlt in range(WINDOW // LANES + 1):
            kv = win_refs[dlt][0]
            kt = _bf(kv[:, 0:LANES])
            v_tiles.append(_bf(kv[:, LANES:2 * LANES]))
            dist = q_pos - ((i - dlt) * LANES + lane)
            mk = (dist >= 0) & (dist < WINDOW) & (i - dlt >= 0)
            bias = jnp.concatenate([tt_ref[dlt, g * R_NSA + r] for r in range(R_NSA)], axis=0)
            s_tiles.append(_dot_t(qg_ref[g], kt) + bias)
            m_tiles.append(jnp.concatenate([mk] * R_NSA, axis=0))
        p_w = _masked_softmax(jnp.concatenate(s_tiles, axis=1), jnp.concatenate(m_tiles, axis=1))
        o_w.append(_dot(_bf(p_w), jnp.concatenate(v_tiles, axis=0)))

    sig = jax.nn.sigmoid(gate_ref[0])
    o_ref[0] = _gate_combine(sig, o_c, o_s, o_w, tq)


def _nsa_prompt(proj, kcvc, tt, tts, tc, sel_map):
    bsz, t, _ = proj.shape
    tq = TQ
    ncp = t // CMP_STRIDE
    n_blk = t // SEL_BLOCK
    map_t = sel_map.T
    nbp = map_t.shape[0]
    rows = R_NSA * tq
    nwin = WINDOW // LANES + 1
    win_specs = [
        pl.BlockSpec((1, tq, 2 * LANES),
                     functools.partial(lambda b, i, d: (b, jnp.maximum(i - d, 0), (C_KVB + 4 * LANES) // (2 * LANES)), d=d))
        for d in range(nwin)]
    far = N_CMP_TILES - 1
    return pl.pallas_call(
        functools.partial(_nsa_prompt_kernel, ncp=ncp, n_blk=n_blk),
        grid=(bsz, t // tq),
        in_specs=[pl.BlockSpec((1, tq, W_NSA), lambda b, i: (b, i, C_QB // W_NSA)),
                  pl.BlockSpec((1, tq, LANES), lambda b, i: (b, i, C_GATE // LANES)),
                  pl.BlockSpec((1, ncp, 2 * LANES), lambda b, i: (b, 0, 0)),
                  pl.BlockSpec((1, t, 2 * LANES), lambda b, i: (b, 0, (C_KVB + 2 * LANES) // (2 * LANES)))]
                 + win_specs +
                 [pl.BlockSpec((nwin, H_NSA, tq, LANES), lambda b, i: (0, 0, 0, 0)),
                  pl.BlockSpec((N_TOK_TILES, H_NSA, tq, LANES), lambda b, i: (0, 0, 0, 0)),
                  pl.BlockSpec((1, H_NSA, tq, LANES), lambda b, i: (i % 16, 0, 0, 0)),
                  pl.BlockSpec((1, H_NSA, tq, LANES), lambda b, i: (jnp.minimum(i % 16 + 16, far), 0, 0, 0)),
                  pl.BlockSpec((1, H_NSA, tq, LANES), lambda b, i: (far, 0, 0, 0)),
                  pl.BlockSpec((nbp, ncp), lambda b, i: (0, 0))],
        out_specs=pl.BlockSpec((1, tq, W_NSA), lambda b, i: (b, i, 0)),
        out_shape=jax.ShapeDtypeStruct((bsz, t, W_NSA), F32),
        scratch_shapes=[pltpu.VMEM((G_NSA, rows, LANES), BF16),
                        pltpu.VMEM((G_NSA, rows, LANES + nbp), BF16),
                        pltpu.VMEM((G_NSA, rows, LANES), F32),
                        pltpu.VMEM((G_NSA, rows, LANES), F32)],
        compiler_params=_cparams(("parallel", "arbitrary")),
    )(proj, proj, kcvc, proj, *([proj] * nwin), tt[:nwin], tts, tc, tc, tc, map_t)


def _nsa_sample_kernel(pt_ref, p_ref, win_ref, cache_ref, w1_ref, pos_ref, w2_ref, map_ref,
                       bsel_ref, bcmp_ref, bwin_ref, o_ref, buf_t, sem, rowbuf, tail_ref, wbuf,
                       *, layer_base, n_pages, nq, n_blk):
    b = pl.program_id(0)
    past = n_pages * PAGE_SIZE
    tk = past + LANES
    ncp = past // CMP_STRIDE
    nbp = map_ref.shape[1]
    rows = R_NSA * nq
    all_rows = G_NSA * rows
    wlen = win_ref.shape[1]
    wpad = wbuf.shape[0]

    def page_copy(j):
        return pltpu.make_async_copy(
            cache_ref.at[layer_base + pt_ref[b, j]],
            buf_t.at[:, :, pl.ds(pl.multiple_of(j * PAGE_SIZE, PAGE_SIZE), PAGE_SIZE)], sem.at[j])

    def start(j, c):
        page_copy(j).start()
        return c

    lax.fori_loop(0, n_pages, start, 0)

    new_nsa = p_ref[0, :, C_KVB - C_QB:C_KVB - C_QB + 4 * LANES]
    zpad = jnp.zeros((LANES - nq, LANES), F32)
    for c in (2, 3):
        buf_t[c, :, past:tk] = jnp.concatenate([new_nsa[:, c * LANES:(c + 1) * LANES], zpad], axis=0).T
    wbuf[0:wlen, :] = win_ref[0]
    wbuf[wlen:wlen + nq, :] = p_ref[0, :, C_KVB - C_QB + 4 * LANES:C_KVB - C_QB + 6 * LANES]
    wbuf[wlen + nq:wpad, :] = jnp.zeros((wpad - wlen - nq, 2 * LANES), F32)

    q = p_ref[0, :, 0:W_NSA] * QK_SCALE
    sig = jax.nn.sigmoid(p_ref[0, :, C_GATE - C_QB:C_GATE - C_QB + LANES])
    q_all = jnp.concatenate([_group_queries(q, g, nq) for g in range(G_NSA)], axis=0)
    q_bf = _bf(q_all)
    q_pos_nb = past + lax.broadcasted_iota(jnp.int32, (G_NSA * nq, nbp), 0) % nq

    kk = lax.broadcasted_iota(jnp.int32, (all_rows, wpad), 1)
    dist_w = wlen + (lax.broadcasted_iota(jnp.int32, (all_rows, wpad), 0) % nq) - kk
    mask_w = (dist_w >= 0) & (dist_w < WINDOW) & (kk < wlen + nq)
    p_w = _masked_softmax(_dot_t(q_bf, _bf(wbuf[:, 0:LANES])) + bwin_ref[...], mask_w)
    o_w = _dot(_bf(p_w), _bf(wbuf[:, LANES:2 * LANES]))

    for j in range(n_pages):
        page_copy(j).wait()
        for c in range(2):
            rowbuf[c, j * PAGE_SIZE:(j + 1) * PAGE_SIZE, :] = buf_t[c, :, j * PAGE_SIZE:(j + 1) * PAGE_SIZE].T
    kc, vc = _compress_rows((rowbuf.at[0], rowbuf.at[1]), ncp, w1_ref, pos_ref, w2_ref, tail_ref)

    n_idx = lax.broadcasted_iota(jnp.int32, (all_rows, ncp), 1)
    p_c = _masked_softmax(_dot3_t(q_all, kc) + bcmp_ref[...], n_idx < ncp - 1)
    o_c = _dot(_bf(p_c), _bf(vc))
    p_sum = []
    for g in range(G_NSA):
        acc = p_c[g * rows:g * rows + nq]
        for r in range(1, R_NSA):
            acc = acc + p_c[g * rows + r * nq:g * rows + (r + 1) * nq]
        p_sum.append(acc)
    imp = _dot2_exact_rhs(jnp.concatenate(p_sum, axis=0), map_ref[...])
    sel = _select_blocks(imp, q_pos_nb, n_blk, axis=1)
    sel_rows = jnp.concatenate([sel[g * nq:(g + 1) * nq] for g in range(G_NSA) for _ in range(R_NSA)], axis=0)

    tok = lax.broadcasted_iota(jnp.int32, (all_rows, tk), 1)
    qi_tok = lax.broadcasted_iota(jnp.int32, (all_rows, tk), 0) % nq
    blk_row = lax.broadcasted_iota(jnp.int32, (nbp, tk), 0)
    blk_tok = lax.broadcasted_iota(jnp.int32, (nbp, tk), 1) // SEL_BLOCK
    expand = jnp.where(blk_row == blk_tok, 1.0, 0.0).astype(BF16)
    mk = jnp.logical_and(_dot(_bf(sel_rows), expand) > 0.5, tok <= past + qi_tok)
    p_s = _masked_softmax(_dot(q_bf, _bf(buf_t[2])) + bsel_ref[...], mk)
    o_s = _dot_t(_bf(p_s), _bf(buf_t[3]))

    per_group = lambda o: [o[g * rows:(g + 1) * rows] for g in range(G_NSA)]
    o_ref[0] = _gate_combine(sig, per_group(o_c), per_group(o_s), per_group(o_w), nq)


def _nsa_sample(proj_s, win_cache, cache_pages, page_table, w1bd, posrow, w2bd, sel_map,
                bsel, bcmp, bwin, layer, nq):
    dbsz, n_pages = page_table.shape
    n_pool = cache_pages.shape[0] // (win_cache.shape[0] // dbsz)
    past = n_pages * PAGE_SIZE
    tk = past + LANES
    ncp = past // CMP_STRIDE
    wlen = win_cache.shape[1]
    wpad = wlen + LANES
    n_blk = -(-(past + nq) // SEL_BLOCK)
    kern = functools.partial(_nsa_sample_kernel, layer_base=layer * n_pool, n_pages=n_pages,
                             nq=nq, n_blk=n_blk)
    const2 = lambda b, pt: (0, 0)
    grid_spec = pltpu.PrefetchScalarGridSpec(
        num_scalar_prefetch=1,
        grid=(dbsz,),
        in_specs=[pl.BlockSpec((1, nq, N_PROJ - C_QB), lambda b, pt: (0, b, 1)),
                  pl.BlockSpec((1, wlen, 2 * LANES), lambda b, pt: (layer * dbsz + b, 0, 0)),
                  pl.BlockSpec(memory_space=pl.ANY),
                  pl.BlockSpec((CMP_LEN, 2, LANES, LANES), lambda b, pt: (0, 0, 0, 0)),
                  pl.BlockSpec((CMP_LEN, 2, 1, LANES), lambda b, pt: (0, 0, 0, 0)),
                  pl.BlockSpec((2, LANES, LANES), lambda b, pt: (0, 0, 0)),
                  pl.BlockSpec(sel_map.shape, const2),
                  pl.BlockSpec(bsel.shape, const2),
                  pl.BlockSpec(bcmp.shape, const2),
                  pl.BlockSpec(bwin.shape, const2)],
        out_specs=pl.BlockSpec((1, nq, W_NSA), lambda b, pt: (0, b, 0)),
        scratch_shapes=[pltpu.VMEM((4, LANES, tk), F32),
                        pltpu.SemaphoreType.DMA((n_pages,)),
                        pltpu.VMEM((2, past, LANES), F32),
                        pltpu.VMEM((2, ncp + SUBLANES, LANES), F32),
                        pltpu.VMEM((wpad, 2 * LANES), F32)],
    )
    return pl.pallas_call(
        kern, grid_spec=grid_spec,
        out_shape=jax.ShapeDtypeStruct((1, dbsz * nq, W_NSA), F32),
        compiler_params=_cparams(("arbitrary",)),
    )(page_table, proj_s, win_cache, cache_pages, w1bd, posrow, w2bd, sel_map, bsel, bcmp, bwin)


def _out_kernel(y_ref, oa_ref, ob_ref, za_ref, zb_ref, gate_ref, norm_ref, w_ref, g_ref, b_ref, o_ref,
                *, alpha):
    def gated_norm(o, z, nrm):
        ms = jnp.mean(o * o, axis=-1, keepdims=True)
        return o * lax.rsqrt(ms + RMS_EPS) * nrm * (z * jax.nn.sigmoid(z))

    ua = gated_norm(oa_ref[0], za_ref[0], norm_ref[:, 0:W_SB])
    ub = gated_norm(ob_ref[0], zb_ref[0], norm_ref[:, W_SB:D_MODEL])
    u = _bf(jnp.concatenate([ua, ub], axis=1))
    mixed = _dot(u, w_ref[...])
    x = alpha * y_ref[0] + gate_ref[0] * mixed
    mu = jnp.mean(x, axis=-1, keepdims=True)
    xc = x - mu
    var = jnp.mean(xc * xc, axis=-1, keepdims=True)
    o_ref[0] = xc * lax.rsqrt(var + LN_EPS) * g_ref[...] + b_ref[...]


def _out_stage(y, o_a, o_b, proj, gate, norm_p, w_out_bf, ln_g, ln_b, alpha):
    bsz, t, d = y.shape
    tm = min(256, t)
    if gate.shape[1] == 1:
        gate_spec = pl.BlockSpec((1, 1, d), lambda b, i: (b, 0, 0))
    else:
        gate_spec = pl.BlockSpec((1, tm, d), lambda b, i: (b, i, 0))
    row = lambda w: pl.BlockSpec((1, w), lambda b, i: (0, 0))
    return pl.pallas_call(
        functools.partial(_out_kernel, alpha=alpha),
        grid=(bsz, t // tm),
        in_specs=[pl.BlockSpec((1, tm, d), lambda b, i: (b, i, 0)),
                  pl.BlockSpec((1, tm, W_SB), lambda b, i: (b, i, 0)),
                  pl.BlockSpec((1, tm, W_NSA), lambda b, i: (b, i, 0)),
                  pl.BlockSpec((1, tm, W_SB), lambda b, i: (b, i, C_ZA // W_SB)),
                  pl.BlockSpec((1, tm, W_NSA), lambda b, i: (b, i, C_ZB // W_NSA)),
                  gate_spec, row(d),
                  pl.BlockSpec((d, d), lambda b, i: (0, 0)),
                  row(d), row(d)],
        out_specs=pl.BlockSpec((1, tm, d), lambda b, i: (b, i, 0)),
        out_shape=jax.ShapeDtypeStruct((bsz, t, d), F32),
        compiler_params=_cparams(("parallel", "parallel")),
    )(y, o_a, o_b, proj, proj, gate, norm_p.reshape(1, d), w_out_bf, ln_g.reshape(1, d), ln_b.reshape(1, d))


def _rel_bucket(dist):
    n = jnp.maximum(dist, 0)
    nf = jnp.maximum(n, 1).astype(F32)
    large = REL_MAX_EXACT + (jnp.log(nf / REL_MAX_EXACT) / math.log(REL_MAX_DIST / REL_MAX_EXACT)
                             * (N_BUCKETS - REL_MAX_EXACT)).astype(jnp.int32)
    large = jnp.minimum(large, N_BUCKETS - 1)
    return jnp.where(n < REL_MAX_EXACT, n, large)


def _nsa_perm(x, axis):
    shp = x.shape
    x = x.reshape(shp[:axis] + (G_NSA, R_NSA, HEAD_DIM) + shp[axis + 1:])
    x = jnp.swapaxes(x, axis, axis + 1)
    return x.reshape(shp)


def _bias_tile_kernel(base_ref, step_ref, thr_ref, rb_ref, o_ref):
    t = pl.program_id(0)
    shape = o_ref.shape[2:]
    dist = (base_ref[t] + lax.broadcasted_iota(jnp.int32, shape, 0)
            - step_ref[t] * lax.broadcasted_iota(jnp.int32, shape, 1))
    for h in range(H_NSA):
        val = jnp.full(shape, rb_ref[0, h], F32)
        for b in range(1, N_BUCKETS):
            val = jnp.where(dist >= thr_ref[b], rb_ref[b, h], val)
        o_ref[0, h] = val


def _bias_tiles(base, step, thr, rel_bias, rows):
    n = len(base)
    grid_spec = pltpu.PrefetchScalarGridSpec(
        num_scalar_prefetch=3,
        grid=(n,),
        in_specs=[pl.BlockSpec(memory_space=pltpu.SMEM)],
        out_specs=pl.BlockSpec((1, H_NSA, rows, LANES), lambda t, *_: (t, 0, 0, 0)),
    )
    return pl.pallas_call(
        _bias_tile_kernel, grid_spec=grid_spec,
        out_shape=jax.ShapeDtypeStruct((n, H_NSA, rows, LANES), F32),
        compiler_params=_cparams(("parallel",)),
    )(jnp.asarray(base, jnp.int32), jnp.asarray(step, jnp.int32), thr, rel_bias)


def _bias_tables(rel_bias, nq, past, wlen):
    bucket = _rel_bucket(jnp.arange(BIAS_DMAX))
    thr = jnp.sum(bucket[None, :] < jnp.arange(N_BUCKETS)[:, None], axis=1).astype(jnp.int32)
    base = [LANES * d for d in range(N_TOK_TILES)] + [LANES * o - (CMP_LEN - 1) for o in range(N_CMP_TILES)]
    step = [1] * N_TOK_TILES + [CMP_STRIDE] * N_CMP_TILES
    tiles = _bias_tiles(base, step, thr, rel_bias, TQ)
    tt, tc = tiles[:N_TOK_TILES], tiles[N_TOK_TILES:]
    n_key, n_cmp, n_win = past // LANES + 1, past // (CMP_STRIDE * LANES), wlen // LANES + 1
    base = ([past - LANES * k for k in range(n_key)]
            + [past - CMP_STRIDE * LANES * j - (CMP_LEN - 1) for j in range(n_cmp)]
            + [wlen - LANES * k for k in range(n_win)])
    step = [1] * n_key + [CMP_STRIDE] * n_cmp + [1] * n_win
    tiles = _bias_tiles(base, step, thr, rel_bias, nq)
    rows = lambda x: jnp.transpose(x, (1, 2, 0, 3)).reshape(H_NSA * nq, x.shape[0] * LANES)
    bsel, bcmp, bwin = rows(tiles[:n_key]), rows(tiles[n_key:n_key + n_cmp]), rows(tiles[n_key + n_cmp:])
    return tt, tc, bsel, bcmp, bwin


def _selection_map(ncp, n_c, n_blk):
    nbp = -(-n_blk // LANES) * LANES
    cs = np.arange(ncp)[:, None] * CMP_STRIDE
    ss = np.arange(nbp)[None, :] * SEL_BLOCK
    ov = np.minimum(cs + CMP_LEN, ss + SEL_BLOCK) - np.maximum(cs, ss)
    m = np.clip(ov, 0, None).astype(np.float32) / CMP_LEN
    m = m * (np.arange(ncp)[:, None] < n_c) * (np.arange(nbp)[None, :] < n_blk)
    return jnp.asarray(m, dtype=BF16)


def _block_diag2(w):
    z = jnp.zeros_like(w)
    return jnp.concatenate([jnp.concatenate([w, z], axis=-1), jnp.concatenate([z, w], axis=-1)], axis=-2)


def kernel(x_prompt, x_sample, cache_sb_kv, cache_nsa_kv, cache_win_kv, page_table, c_prompt, c_sample,
           ln_in_g, ln_in_b, w_ada, b_ada, w_in, w_cmp1, w_cmp2, pos_cmp, norm_grp, w_out, ln_g, ln_b,
           rel_bias):
    bsz, seq, d = x_prompt.shape
    dbsz, nq, _ = x_sample.shape
    depth = w_ada.shape[0]
    n_pool = cache_sb_kv.shape[1]
    n_pages = page_table.shape[1]
    past = n_pages * PAGE_SIZE
    wlen = cache_win_kv.shape[2]
    alpha = (2 * depth) ** 0.25
    assert d == D_MODEL and seq % (16 * TQ) == 0 and past % (16 * TQ) == 0 and wlen == WINDOW
    assert seq // SEL_BLOCK >= SEL_TOPK and nq == SUBLANES

    w_qb = _nsa_perm(w_in[:, :, 2048:2560], 2)
    w_zb = _nsa_perm(w_in[:, :, 3352:3864], 2)
    pad = lambda n: jnp.zeros((depth, d, n), F32)
    w_in_p = _bf(jnp.concatenate(
        [w_in[:, :, 0:2048], w_qb, w_in[:, :, 2560:3328], w_in[:, :, 3328:3352], pad(C_ZB - C_GATE - 3 * H_NSA),
         w_zb], axis=2))
    norm_p = jnp.concatenate([norm_grp[:, :W_SB], _nsa_perm(norm_grp[:, W_SB:], 1)], axis=1)
    w_out_p = _bf(jnp.concatenate([w_out[:, :W_SB], _nsa_perm(w_out[:, W_SB:], 1)], axis=1))
    w1bd = _bf(jnp.swapaxes(_block_diag2(w_cmp1), 1, 2))
    w2bd = _bf(_block_diag2(w_cmp2))
    posrow = jnp.swapaxes(jnp.concatenate([pos_cmp, pos_cmp], axis=-1), 1, 2)[:, :, :, None, :]
    tt, tc, bsel, bcmp, bwin = _bias_tables(rel_bias, nq, past, wlen)
    tts = (tt - tt[N_TOK_TILES - 1:]) * LOG2E
    map_p = _selection_map(seq // CMP_STRIDE, (seq - CMP_LEN) // CMP_STRIDE + 1, seq // SEL_BLOCK)
    n_blk_s = -(-(past + nq) // SEL_BLOCK)
    map_s = _selection_map(past // CMP_STRIDE, (past + nq - CMP_LEN) // CMP_STRIDE + 1, n_blk_s)

    n_c = bsz + dbsz
    c_rows = -(-n_c // SUBLANES) * SUBLANES
    c_all = jnp.concatenate([c_prompt, c_sample, jnp.zeros((c_rows - n_c, d), F32)], axis=0)
    mod = _ada_mod(c_all, w_ada, b_ada)

    y_p = _layer_norm(x_prompt.reshape(bsz * seq, d), ln_in_g, ln_in_b).reshape(bsz, seq, d)
    y_s = _layer_norm(x_sample.reshape(dbsz * nq, d), ln_in_g, ln_in_b).reshape(1, dbsz * nq, d)

    sb_pages = jnp.transpose(cache_sb_kv, (0, 1, 3, 4, 5, 2)).reshape(depth * n_pool, 2, H_SB, HEAD_DIM, PAGE_SIZE)
    nsa_pages = jnp.transpose(cache_nsa_kv, (0, 1, 3, 4, 5, 2)).reshape(depth * n_pool, 4, KV_NSA, PAGE_SIZE)
    win_cache = cache_win_kv.reshape(depth * dbsz, wlen, 2 * KV_NSA)

    p_sb, p_nsa, p_win, s_sb, s_nsa, s_win = [], [], [], [], [], []
    for l in range(depth):
        mp = mod[l, :bsz].reshape(bsz, 1, 3 * d)
        proj = _in_proj(y_p, mp[:, :, 0:d], mp[:, :, d:2 * d], w_in_p[l])
        o_a = _sb_prompt(proj)
        kcvc = _compress_prompt(proj, w1bd[l], posrow[l], w2bd[l])
        o_b = _nsa_prompt(proj, kcvc, tt, tts, tc, map_p)
        p_sb.append(proj[:, :, C_KA:C_ZA].reshape(bsz, seq, 2, H_SB, HEAD_DIM))
        p_nsa.append(proj[:, :, C_KVB:C_KVB + 4 * KV_NSA].reshape(bsz, seq, 4, G_NSA, HEAD_DIM))
        p_win.append(proj[:, seq - min(WINDOW, seq):, C_KVB + 4 * KV_NSA:C_KVB + 6 * KV_NSA]
                     .reshape(bsz, min(WINDOW, seq), 2, G_NSA, HEAD_DIM))
        y_p = _out_stage(y_p, o_a, o_b, proj, mp[:, :, 2 * d:3 * d], norm_p[l], w_out_p[l], ln_g[l], ln_b[l],
                         alpha)
        ms = jnp.repeat(mod[l, bsz:bsz + dbsz], nq, axis=0).reshape(1, dbsz * nq, 3 * d)
        proj_s = _in_proj(y_s, ms[:, :, 0:d], ms[:, :, d:2 * d], w_in_p[l])
        o_a = _sb_sample(proj_s, sb_pages, page_table, l * n_pool, nq)
        o_b = _nsa_sample(proj_s, win_cache, nsa_pages, page_table, w1bd[l], posrow[l], w2bd[l], map_s,
                          bsel, bcmp, bwin, l, nq)
        ps = proj_s.reshape(dbsz, nq, N_PROJ)
        s_sb.append(ps[:, :, C_KA:C_ZA].reshape(dbsz, nq, 2, H_SB, HEAD_DIM))
        s_nsa.append(ps[:, :, C_KVB:C_KVB + 4 * KV_NSA].reshape(dbsz, nq, 4, G_NSA, HEAD_DIM))
        new_win = ps[:, :, C_KVB + 4 * KV_NSA:C_KVB + 6 * KV_NSA].reshape(dbsz, nq, 2, G_NSA, HEAD_DIM)
        win_all = jnp.concatenate([cache_win_kv[l], new_win], axis=1)
        s_win.append(win_all[:, win_all.shape[1] - min(WINDOW, past + nq):])
        y_s = _out_stage(y_s, o_a, o_b, proj_s, ms[:, :, 2 * d:3 * d], norm_p[l], w_out_p[l], ln_g[l],
                         ln_b[l], alpha)

    return (y_p, y_s.reshape(dbsz, nq, d), jnp.stack(p_sb), jnp.stack(p_nsa), jnp.stack(p_win),
            jnp.stack(s_sb), jnp.stack(s_nsa), jnp.stack(s_win))
```

```python
import functools
import math

import numpy as np
import jax
import jax.numpy as jnp
from jax import lax
from jax.experimental import pallas as pl
from jax.experimental.pallas import tpu as pltpu

F32 = jnp.float32
BF16 = jnp.bfloat16

D_MODEL = 1024
HEAD_DIM = 64
W_SB = D_MODEL // 2
W_NSA = D_MODEL - W_SB
H_SB = W_SB // HEAD_DIM
H_NSA = W_NSA // HEAD_DIM
G_NSA = 2
R_NSA = H_NSA // G_NSA
KV_NSA = G_NSA * HEAD_DIM
CMP_LEN = 32
CMP_STRIDE = 16
SEL_BLOCK = 64
SEL_TOPK = 16
WINDOW = 512
PAGE_SIZE = 128
N_BUCKETS = 32
REL_MAX_EXACT = N_BUCKETS // 2
REL_MAX_DIST = 1024
LN_EPS = 1e-5
RMS_EPS = 1e-6
NEG = -1e30
MASK_VALUE = -1e30
FORCE_SCORE = 1e4
QK_SCALE = HEAD_DIM ** -0.5
LOG2E = math.log2(math.e)

LANES = 128
SUBLANES = 8
VMEM_LIMIT = 56 * 1024 * 1024

C_QA, C_KA, C_VA, C_ZA, C_QB, C_KVB = 0, 512, 1024, 1536, 2048, 2560
C_GATE, C_ZB, N_PROJ = 3328, 3584, 4096
SB_DEAD = -104.0
BIAS_DMAX = 1024
N_TOK_TILES = 9
N_CMP_TILES = 24
TQ = 128


def _bf(x):
    return x.astype(BF16)


def _dot(a, b):
    return jnp.dot(a, b, preferred_element_type=F32)


def _dot_t(a, b):
    return lax.dot_general(a, b, (((1,), (1,)), ((), ())), preferred_element_type=F32)


def _split(x):
    hi = _bf(x)
    lo = _bf(x - hi.astype(F32))
    return hi, lo


def _dot3_t(a, b):
    ah, al = _split(a)
    bh, bl = _split(b)
    return _dot_t(ah, bh) + _dot_t(ah, bl) + _dot_t(al, bh)


def _dot3(a, b):
    ah, al = _split(a)
    bh, bl = _split(b)
    return _dot(ah, bh) + _dot(ah, bl) + _dot(al, bh)


def _dot2_exact_rhs(a, b_bf):
    ah, al = _split(a)
    return _dot(ah, b_bf) + _dot(al, b_bf)


def _softplus(z):
    return jnp.maximum(z, 0.0) + jnp.log1p(jnp.exp(-jnp.abs(z)))


def _cparams(sem):
    return pltpu.CompilerParams(dimension_semantics=sem, vmem_limit_bytes=VMEM_LIMIT)


def _ln_kernel(x_ref, g_ref, b_ref, o_ref):
    x = x_ref[...]
    mu = jnp.mean(x, axis=-1, keepdims=True)
    xc = x - mu
    var = jnp.mean(xc * xc, axis=-1, keepdims=True)
    o_ref[...] = xc * lax.rsqrt(var + LN_EPS) * g_ref[...] + b_ref[...]


def _layer_norm(x2d, g, b):
    rows, d = x2d.shape
    tm = min(512, rows)
    return pl.pallas_call(
        _ln_kernel,
        grid=(rows // tm,),
        in_specs=[pl.BlockSpec((tm, d), lambda i: (i, 0)),
                  pl.BlockSpec((1, d), lambda i: (0, 0)),
                  pl.BlockSpec((1, d), lambda i: (0, 0))],
        out_specs=pl.BlockSpec((tm, d), lambda i: (i, 0)),
        out_shape=jax.ShapeDtypeStruct((rows, d), F32),
        compiler_params=_cparams(("parallel",)),
    )(x2d, g.reshape(1, d), b.reshape(1, d))


def _mod_kernel(c_ref, w_ref, b_ref, o_ref):
    c = c_ref[...]
    s = c * jax.nn.sigmoid(c)
    o_ref[0] = _dot3(s, w_ref[0]) + b_ref[0]


def _ada_mod(c_all, w_ada, b_ada):
    depth, d, n3 = w_ada.shape
    rows = c_all.shape[0]
    tn = 1024
    return pl.pallas_call(
        _mod_kernel,
        grid=(depth, n3 // tn),
        in_specs=[pl.BlockSpec((rows, d), lambda l, j: (0, 0)),
                  pl.BlockSpec((1, d, tn), lambda l, j: (l, 0, j)),
                  pl.BlockSpec((1, 1, tn), lambda l, j: (l, 0, j))],
        out_specs=pl.BlockSpec((1, rows, tn), lambda l, j: (l, 0, j)),
        out_shape=jax.ShapeDtypeStruct((depth, rows, n3), F32),
        compiler_params=_cparams(("parallel", "parallel")),
    )(c_all, w_ada, b_ada.reshape(depth, 1, n3))


def _inproj_kernel(y_ref, sh_ref, sc_ref, w_ref, o_ref):
    h = y_ref[0] * (1.0 + sc_ref[0]) + sh_ref[0]
    hb = _bf(h)
    chunk = 512
    for c0 in range(0, N_PROJ, chunk):
        o_ref[0, :, c0:c0 + chunk] = _dot(hb, w_ref[:, c0:c0 + chunk])


def _in_proj(y, shift, scale, w_bf):
    bsz, t, d = y.shape
    tm = min(256, t)
    tmod = shift.shape[1]
    if tmod == 1:
        mod_spec = pl.BlockSpec((1, 1, d), lambda b, i: (b, 0, 0))
    else:
        mod_spec = pl.BlockSpec((1, tm, d), lambda b, i: (b, i, 0))
    return pl.pallas_call(
        _inproj_kernel,
        grid=(bsz, t // tm),
        in_specs=[pl.BlockSpec((1, tm, d), lambda b, i: (b, i, 0)), mod_spec, mod_spec,
                  pl.BlockSpec((d, N_PROJ), lambda b, i: (0, 0))],
        out_specs=pl.BlockSpec((1, tm, N_PROJ), lambda b, i: (b, i, 0)),
        out_shape=jax.ShapeDtypeStruct((bsz, t, N_PROJ), F32),
        compiler_params=_cparams(("parallel", "parallel")),
    )(y, shift, scale, w_bf)


def _sb_tile(qh, kt, vt, mask, carry, tri):
    a, carry = _sb_weights(_dot_t(qh, kt), mask, carry, tri)
    return _dot(_bf(a), vt), carry


def _sb_weights(z, mask, carry, tri):
    lk = -_softplus(z)
    if mask is not None:
        lk = jnp.where(mask, lk, 0.0)
    after = []
    for blk in reversed(range(z.shape[1] // LANES)):
        lk_b = lk[:, blk * LANES:(blk + 1) * LANES]
        hi, lo = _split(lk_b)
        after.insert(0, _dot(hi, tri) + _dot(lo, tri) + carry)
        carry = carry + jnp.sum(lk_b, axis=1, keepdims=True)
    after = jnp.concatenate(after, axis=1) if len(after) > 1 else after[0]
    a = jnp.exp(z + lk + after)
    if mask is not None:
        a = jnp.where(mask, a, 0.0)
    return a, carry


def _tri_newer():
    r = lax.broadcasted_iota(jnp.int32, (LANES, LANES), 0)
    c = lax.broadcasted_iota(jnp.int32, (LANES, LANES), 1)
    return jnp.where(r > c, 1.0, 0.0).astype(BF16)


def _sb_prompt_kernel(q_ref, k_ref, v_ref, o_ref, *, tq):
    i = pl.program_id(2)
    q0 = i * tq
    q = q_ref[0] * QK_SCALE
    lane = lax.broadcasted_iota(jnp.int32, (tq, LANES), 1)
    row_pos = q0 + lax.broadcasted_iota(jnp.int32, (tq, LANES), 0)
    tri = _tri_newer()
    qh = _bf(jnp.concatenate([jnp.where(lane < HEAD_DIM, q, 0.0), jnp.where(lane >= HEAD_DIM, q, 0.0)], axis=0))
    def chunk(c, mask, carry):
        k0 = pl.multiple_of(c * tq, tq)
        kt = _bf(k_ref[0, pl.ds(k0, tq), :])
        vt = _bf(v_ref[0, pl.ds(k0, tq), :])
        return _sb_tile(qh, kt, vt, mask, carry, tri)

    key_pos = q0 + lax.broadcasted_iota(jnp.int32, (2 * tq, tq), 1)
    qry_pos = q0 + lax.broadcasted_iota(jnp.int32, (2 * tq, tq), 0) % tq
    acc, carry = chunk(i, key_pos < qry_pos, jnp.zeros((2 * tq, LANES), F32))

    def cond(s):
        c, alive, _, _ = s
        return jnp.logical_and(c >= 0, alive)

    def body(s):
        c, _, carry, acc = s
        contrib, carry = chunk(c, None, carry)
        return c - 1, jnp.max(carry) > SB_DEAD, carry, acc + contrib

    _, _, _, acc = lax.while_loop(cond, body, (i - 1, jnp.max(carry) > SB_DEAD, carry, acc))
    o_ref[0] = jnp.where(lane < HEAD_DIM, acc[0:tq], acc[tq:2 * tq])


def _sb_prompt(proj, tq=256):
    bsz, t, _ = proj.shape
    tq = min(tq, t)
    npair = W_SB // LANES
    return pl.pallas_call(
        functools.partial(_sb_prompt_kernel, tq=tq),
        grid=(bsz, npair, t // tq),
        in_specs=[pl.BlockSpec((1, tq, LANES), lambda b, p, i: (b, i, C_QA // LANES + p)),
                  pl.BlockSpec((1, t, LANES), lambda b, p, i: (b, 0, C_KA // LANES + p)),
                  pl.BlockSpec((1, t, LANES), lambda b, p, i: (b, 0, C_VA // LANES + p))],
        out_specs=pl.BlockSpec((1, tq, LANES), lambda b, p, i: (b, i, p)),
        out_shape=jax.ShapeDtypeStruct((bsz, t, W_SB), F32),
        compiler_params=_cparams(("parallel", "parallel", "arbitrary")),
    )(proj, proj, proj)


def _sb_sample_kernel(pt_ref, p_ref, last_ref, cache_ref, o_ref, buf, sem, carry_ref, acc_ref,
                      *, layer_base, n_pages, nq):
    b = pl.program_id(0)
    rows = H_SB * nq
    q = p_ref[0, :, C_QA:C_QA + W_SB] * QK_SCALE
    lane_c = lax.broadcasted_iota(jnp.int32, (rows, W_SB), 1)
    row_c = lax.broadcasted_iota(jnp.int32, (rows, W_SB), 0)
    own = (lane_c // HEAD_DIM) == (row_c // nq)
    qbd = _bf(jnp.where(own, jnp.concatenate([q] * H_SB, axis=0), 0.0))
    tri = _tri_newer()
    key = lax.broadcasted_iota(jnp.int32, (rows, LANES), 1)
    qi = lax.broadcasted_iota(jnp.int32, (rows, LANES), 0) % nq

    pad = jnp.zeros((LANES - nq, W_SB), F32)
    k_new = _bf(jnp.concatenate([p_ref[0, :, C_KA:C_KA + W_SB], pad], axis=0))
    v_new = _bf(jnp.concatenate([p_ref[0, :, C_VA:C_VA + W_SB], pad], axis=0))
    contrib, carry = _sb_tile(qbd, k_new, v_new, key < qi, jnp.zeros((rows, LANES), F32), tri)
    carry_ref[...] = carry
    acc_ref[...] = contrib

    def page_tile(kv_ref):
        kt_t = _bf(kv_ref[0].reshape(W_SB, PAGE_SIZE))
        vt_t = _bf(kv_ref[1].reshape(W_SB, PAGE_SIZE))
        a, carry = _sb_weights(_dot(qbd, kt_t), None, carry_ref[...], tri)
        acc_ref[...] += _dot_t(_bf(a), vt_t)
        carry_ref[...] = carry

    @pl.when(jnp.max(carry_ref[...]) > SB_DEAD)
    def _():
        page_tile(last_ref.at[0])

    def cond(c):
        j, alive = c
        return jnp.logical_and(j >= 0, alive)

    def body(c):
        j, _ = c
        page = layer_base + pt_ref[b, j]
        cp = pltpu.make_async_copy(cache_ref.at[page], buf, sem)
        cp.start()
        cp.wait()
        page_tile(buf)
        return j - 1, jnp.max(carry_ref[...]) > SB_DEAD

    lax.while_loop(cond, body, (n_pages - 2, jnp.max(carry_ref[...]) > SB_DEAD))

    acc = jnp.where(own, acc_ref[...], 0.0)
    o = acc[0:nq]
    for h in range(1, H_SB):
        o = o + acc[h * nq:(h + 1) * nq]
    o_ref[0] = o


def _sb_sample(proj_s, cache_pages, page_table, layer_base, nq):
    dbsz, n_pages = page_table.shape
    rows = H_SB * nq
    page_shape = cache_pages.shape[1:]
    kern = functools.partial(_sb_sample_kernel, layer_base=layer_base, n_pages=n_pages, nq=nq)
    grid_spec = pltpu.PrefetchScalarGridSpec(
        num_scalar_prefetch=1,
        grid=(dbsz,),
        in_specs=[pl.BlockSpec((1, nq, C_QB), lambda b, pt: (0, b, 0)),
                  pl.BlockSpec((1,) + page_shape,
                               lambda b, pt: (layer_base + pt[b, n_pages - 1], 0, 0, 0, 0)),
                  pl.BlockSpec(memory_space=pl.ANY)],
        out_specs=pl.BlockSpec((1, nq, W_SB), lambda b, pt: (0, b, 0)),
        scratch_shapes=[pltpu.VMEM(page_shape, F32),
                        pltpu.SemaphoreType.DMA(()),
                        pltpu.VMEM((rows, LANES), F32),
                        pltpu.VMEM((rows, W_SB), F32)],
    )
    return pl.pallas_call(
        kern, grid_spec=grid_spec,
        out_shape=jax.ShapeDtypeStruct((1, dbsz * nq, W_SB), F32),
        compiler_params=_cparams(("arbitrary",)),
    )(page_table, proj_s, cache_pages, cache_pages)


def _compress_rows(x_refs, ncp, w1_ref, bias_ref, w2_ref, tail_ref):
    outs = []
    for c, x_ref in enumerate(x_refs):
        acc = jnp.zeros((ncp, 2 * LANES), F32)
        for m in range(CMP_STRIDE // 2):
            r = jnp.concatenate([x_ref[pl.ds(2 * m, ncp, stride=CMP_STRIDE), :],
                                 x_ref[pl.ds(2 * m + 1, ncp, stride=CMP_STRIDE), :]], axis=1)
            acc = acc + _dot(_bf(r), w1_ref[m, c])
        acc = acc + bias_ref[c]
        tail_ref[c, 0:ncp, :] = acc[:, LANES:2 * LANES]
        tail_ref[c, ncp:ncp + SUBLANES, :] = jnp.zeros((SUBLANES, LANES), F32)
        hid = jax.nn.gelu(acc[:, 0:LANES] + tail_ref[c, pl.ds(1, ncp), :])
        outs.append(_dot(_bf(hid), w2_ref[c]))
    return outs


def _compress_kernel(xk_ref, xv_ref, w1_ref, pos_ref, w2_ref, o_ref, tail_ref, *, ncp):
    kc, vc = _compress_rows((xk_ref.at[0], xv_ref.at[0]), ncp, w1_ref, pos_ref, w2_ref, tail_ref)
    o_ref[0, :, 0:LANES] = kc
    o_ref[0, :, LANES:2 * LANES] = vc


def _compress_prompt(proj, w1bd, posrow, w2bd):
    bsz, t, _ = proj.shape
    ncp = t // CMP_STRIDE
    return pl.pallas_call(
        functools.partial(_compress_kernel, ncp=ncp),
        grid=(bsz,),
        in_specs=[pl.BlockSpec((1, t, LANES), lambda b: (b, 0, C_KVB // LANES)),
                  pl.BlockSpec((1, t, LANES), lambda b: (b, 0, C_KVB // LANES + 1)),
                  pl.BlockSpec((CMP_STRIDE // 2, 2, 2 * LANES, 2 * LANES), lambda b: (0, 0, 0, 0)),
                  pl.BlockSpec((2, 1, 2 * LANES), lambda b: (0, 0, 0)),
                  pl.BlockSpec((2, LANES, LANES), lambda b: (0, 0, 0))],
        out_specs=pl.BlockSpec((1, ncp, 2 * LANES), lambda b: (b, 0, 0)),
        out_shape=jax.ShapeDtypeStruct((bsz, ncp, 2 * LANES), F32),
        scratch_shapes=[pltpu.VMEM((2, ncp + SUBLANES, LANES), F32)],
        compiler_params=_cparams(("parallel",)),
    )(proj, proj, w1bd, posrow, w2bd)


def _group_queries(q, g, nq):
    lane = lax.broadcasted_iota(jnp.int32, (nq, LANES), 1)
    in_g = (lane >= HEAD_DIM) if g else (lane < HEAD_DIM)
    return jnp.concatenate(
        [jnp.where(in_g, q[:, r * LANES:(r + 1) * LANES], 0.0) for r in range(R_NSA)], axis=0)


def _masked_softmax(s, mask):
    s = jnp.where(mask, s, NEG)
    m = jnp.max(s, axis=1, keepdims=True)
    p = jnp.where(mask, jnp.exp(s - m), 0.0)
    l = jnp.sum(p, axis=1, keepdims=True)
    return p / jnp.maximum(l, 1e-30)


def _select_blocks(imp, q_pos, n_blk, axis):
    nb = imp.shape[axis]
    blk = lax.broadcasted_iota(jnp.int32, imp.shape, axis)
    blk_f = blk.astype(F32)
    cur = q_pos // SEL_BLOCK
    forced = (blk == 0) | (blk == cur) | (blk == cur - 1)
    valid = blk * SEL_BLOCK <= q_pos
    score = jnp.where(forced, FORCE_SCORE, jnp.where(valid, imp, -1.0))
    score = jnp.where(blk < n_blk, score, -jnp.inf)
    sel = jnp.zeros(imp.shape, F32)
    for _ in range(SEL_TOPK):
        m = jnp.max(score, axis=axis, keepdims=True)
        first = jnp.min(jnp.where(score == m, blk_f, float(nb)), axis=axis, keepdims=True)
        pick = blk_f == first
        sel = jnp.where(pick, 1.0, sel)
        score = jnp.where(pick, -jnp.inf, score)
    return sel


def _gate_combine(sig, o_c, o_s, o_w, nq):
    lane = lax.broadcasted_iota(jnp.int32, (nq, LANES), 1)
    slabs = []
    for r in range(R_NSA):
        per_g = []
        for g in range(G_NSA):
            h = g * R_NSA + r
            rows = slice(r * nq, (r + 1) * nq)
            acc = sig[:, h:h + 1] * o_c[g][rows]
            acc = acc + sig[:, H_NSA + h:H_NSA + h + 1] * o_s[g][rows]
            acc = acc + sig[:, 2 * H_NSA + h:2 * H_NSA + h + 1] * o_w[g][rows]
            per_g.append(acc)
        slabs.append(jnp.where(lane < HEAD_DIM, per_g[0], per_g[1]))
    return jnp.concatenate(slabs, axis=1)


def _nsa_prompt_kernel(q_ref, gate_ref, kcvc_ref, selkv_ref, w0_ref, w1_ref, w2_ref, w3_ref, w4_ref,
                       tt_ref, tts_ref, tca_ref, tcb_ref, tcf_ref, mapt_ref, o_ref,
                       qg_ref, qa_ref, m_ref, acc_ref, *, ncp, n_blk):
    tq = TQ
    i = pl.program_id(1)
    q0 = i * tq
    rows = R_NSA * tq
    q = q_ref[0] * QK_SCALE
    lane = lax.broadcasted_iota(jnp.int32, (tq, LANES), 1)
    q_pos = q0 + lax.broadcasted_iota(jnp.int32, (tq, LANES), 0)
    nbp = mapt_ref.shape[0]
    q_pos_t = q0 + lax.broadcasted_iota(jnp.int32, (nbp, tq), 1)
    win_refs = (w0_ref, w1_ref, w2_ref, w3_ref, w4_ref)
    n_ctile = ncp // LANES
    jn_a = i // 16
    o_c, o_s, o_w = [], [], []

    for g in range(G_NSA):
        qg = _group_queries(q, g, tq)
        qg_ref[g] = _bf(qg)
        s = _dot3_t(qg, kcvc_ref[0, :, 0:LANES])
        n_idx = lax.broadcasted_iota(jnp.int32, (tq, ncp), 1)
        qp_c = q0 + lax.broadcasted_iota(jnp.int32, (tq, ncp), 0)
        mask_c = (n_idx * CMP_STRIDE + CMP_LEN - 1) <= qp_c
        bias_rows = []
        for r in range(R_NSA):
            h = g * R_NSA + r
            tiles = []
            for jn in range(n_ctile):
                t = jnp.where(jn == jn_a, tca_ref[0, h],
                              jnp.where(jn == jn_a - 1, tcb_ref[0, h], tcf_ref[0, h]))
                tiles.append(t)
            bias_rows.append(jnp.concatenate(tiles, axis=1) if n_ctile > 1 else tiles[0])
        bias_c = jnp.concatenate(bias_rows, axis=0)
        mask_c4 = jnp.concatenate([mask_c] * R_NSA, axis=0)
        p_c = _masked_softmax(s + bias_c, mask_c4)
        o_c.append(_dot(_bf(p_c), _bf(kcvc_ref[0, :, LANES:2 * LANES])))
        p_sum = p_c[0:tq]
        for r in range(1, R_NSA):
            p_sum = p_sum + p_c[r * tq:(r + 1) * tq]
        p_hi, p_lo = _split(p_sum)
        imp_t = _dot_t(mapt_ref[...], p_hi) + _dot_t(mapt_ref[...], p_lo)
        sel_t = _select_blocks(imp_t, q_pos_t, n_blk, axis=0)
        not_sel = _bf(1.0 - sel_t.T)
        qa_ref[g] = jnp.concatenate([_bf(qg * LOG2E), jnp.concatenate([not_sel] * R_NSA, axis=0)], axis=1)

    m_ref[...] = jnp.full((G_NSA, rows, LANES), NEG, F32)
    acc_ref[...] = jnp.zeros((G_NSA, rows, LANES), F32)
    causal4 = jnp.concatenate([(q0 + lane) <= q_pos] * R_NSA, axis=0)

    def sel_chunk(j0, nt, near, causal=False):
        width = nt * LANES
        k0 = pl.multiple_of(j0 * LANES, LANES)
        kv = selkv_ref[0, pl.ds(k0, width), :]
        v = kv[:, LANES:2 * LANES]
        v_lane = lax.broadcasted_iota(jnp.int32, (width, LANES), 1)
        key_blk = 2 * j0 + lax.broadcasted_iota(jnp.int32, (width, nbp), 0) // SEL_BLOCK
        blk_col = lax.broadcasted_iota(jnp.int32, (width, nbp), 1)
        k_aug = jnp.concatenate([_bf(kv[:, 0:LANES]),
                                 jnp.where(key_blk == blk_col, MASK_VALUE, 0.0).astype(BF16)], axis=1)
        scs = [_dot_t(qa_ref[g], k_aug) for g in range(G_NSA)]
        for g in range(G_NSA):
            vt = _bf(jnp.where((v_lane < HEAD_DIM) if g else (v_lane >= HEAD_DIM), 1.0, v))
            sc = scs[g]
            if near:
                bias_t = []
                for t in range(nt):
                    d = jnp.minimum(i - j0 - t, N_TOK_TILES - 1)
                    bias_t.append(jnp.concatenate([tts_ref[d, g * R_NSA + r] for r in range(R_NSA)], axis=0))
                sc = sc + (jnp.concatenate(bias_t, axis=1) if nt > 1 else bias_t[0])
            if causal:
                sc = jnp.where(causal4, sc, NEG)
            m_old = m_ref[g]
            m_new = jnp.maximum(m_old, jnp.max(sc, axis=1, keepdims=True))
            p = jnp.exp2(sc - (jnp.concatenate([m_new] * nt, axis=1) if nt > 1 else m_new))
            acc_ref[g] = jnp.exp2(m_old - m_new) * acc_ref[g] + _dot(_bf(p), vt)
            m_ref[g] = m_new

    sel_chunk(i, 1, True, causal=True)
    near_pairs = (N_TOK_TILES - 1) // 2

    def near_body(p, carry):
        sel_chunk(i - 2 - 2 * p, 2, True)
        return carry

    lax.fori_loop(0, jnp.minimum(near_pairs, i // 2), near_body, 0)

    far_pairs = jnp.maximum(i // 2 - near_pairs, 0)

    def far_body(c, carry):
        sel_chunk(i - 4 - 2 * (near_pairs + 2 * c), 4, False)
        return carry

    lax.fori_loop(0, far_pairs // 2, far_body, 0)

    @pl.when(far_pairs % 2 == 1)
    def _():
        sel_chunk(i % 2, 2, False)

    @pl.when(i % 2 == 1)
    def _():
        sel_chunk(0, 1, True)

    for g in range(G_NSA):
        acc = acc_ref[g]
        den = acc[:, 0:1] if g else acc[:, LANES - 1:LANES]
        o_s.append(acc / den)

    for g in range(G_NSA):
        s_tiles, v_tiles, m_tiles = [], [], []
        for dlt in range(WINDOW // LANES + 1):
            kv = win_refs[dlt][0]
            kt = _bf(kv[:, 0:LANES])
            v_tiles.append(_bf(kv[:, LANES:2 * LANES]))
            dist = q_pos - ((i - dlt) * LANES + lane)
            mk = (dist >= 0) & (dist < WINDOW) & (i - dlt >= 0)
            bias = jnp.concatenate([tt_ref[dlt, g * R_NSA + r] for r in range(R_NSA)], axis=0)
            s_tiles.append(_dot_t(qg_ref[g], kt) + bias)
            m_tiles.append(jnp.concatenate([mk] * R_NSA, axis=0))
        p_w = _masked_softmax(jnp.concatenate(s_tiles, axis=1), jnp.concatenate(m_tiles, axis=1))
        o_w.append(_dot(_bf(p_w), jnp.concatenate(v_tiles, axis=0)))

    sig = jax.nn.sigmoid(gate_ref[0])
    o_ref[0] = _gate_combine(sig, o_c, o_s, o_w, tq)


def _nsa_prompt(proj, kcvc, tt, tts, tc, sel_map):
    bsz, t, _ = proj.shape
    tq = TQ
    ncp = t // CMP_STRIDE
    n_blk = t // SEL_BLOCK
    map_t = sel_map.T
    nbp = map_t.shape[0]
    rows = R_NSA * tq
    nwin = WINDOW // LANES + 1
    win_specs = [
        pl.BlockSpec((1, tq, 2 * LANES),
                     functools.partial(lambda b, i, d: (b, jnp.maximum(i - d, 0), (C_KVB + 4 * LANES) // (2 * LANES)), d=d))
        for d in range(nwin)]
    far = N_CMP_TILES - 1
    return pl.pallas_call(
        functools.partial(_nsa_prompt_kernel, ncp=ncp, n_blk=n_blk),
        grid=(bsz, t // tq),
        in_specs=[pl.BlockSpec((1, tq, W_NSA), lambda b, i: (b, i, C_QB // W_NSA)),
                  pl.BlockSpec((1, tq, LANES), lambda b, i: (b, i, C_GATE // LANES)),
                  pl.BlockSpec((1, ncp, 2 * LANES), lambda b, i: (b, 0, 0)),
                  pl.BlockSpec((1, t, 2 * LANES), lambda b, i: (b, 0, (C_KVB + 2 * LANES) // (2 * LANES)))]
                 + win_specs +
                 [pl.BlockSpec((nwin, H_NSA, tq, LANES), lambda b, i: (0, 0, 0, 0)),
                  pl.BlockSpec((N_TOK_TILES, H_NSA, tq, LANES), lambda b, i: (0, 0, 0, 0)),
                  pl.BlockSpec((1, H_NSA, tq, LANES), lambda b, i: (i % 16, 0, 0, 0)),
                  pl.BlockSpec((1, H_NSA, tq, LANES), lambda b, i: (jnp.minimum(i % 16 + 16, far), 0, 0, 0)),
                  pl.BlockSpec((1, H_NSA, tq, LANES), lambda b, i: (far, 0, 0, 0)),
                  pl.BlockSpec((nbp, ncp), lambda b, i: (0, 0))],
        out_specs=pl.BlockSpec((1, tq, W_NSA), lambda b, i: (b, i, 0)),
        out_shape=jax.ShapeDtypeStruct((bsz, t, W_NSA), F32),
        scratch_shapes=[pltpu.VMEM((G_NSA, rows, LANES), BF16),
                        pltpu.VMEM((G_NSA, rows, LANES + nbp), BF16),
                        pltpu.VMEM((G_NSA, rows, LANES), F32),
                        pltpu.VMEM((G_NSA, rows, LANES), F32)],
        compiler_params=_cparams(("parallel", "arbitrary")),
    )(proj, proj, kcvc, proj, *([proj] * nwin), tt[:nwin], tts, tc, tc, tc, map_t)


def _nsa_sample_kernel(pt_ref, p_ref, win_ref, cache_ref, w1_ref, pos_ref, w2_ref, map_ref,
                       bsel_ref, bcmp_ref, bwin_ref, o_ref, buf_t, sem, rowbuf, tail_ref, wbuf,
                       *, layer_base, n_pages, nq, n_blk):
    b = pl.program_id(0)
    past = n_pages * PAGE_SIZE
    tk = past + LANES
    ncp = past // CMP_STRIDE
    nbp = map_ref.shape[0]
    rows = R_NSA * nq
    all_rows = G_NSA * rows
    wlen = win_ref.shape[1]
    wpad = wbuf.shape[0]

    def page_copy(j):
        return pltpu.make_async_copy(
            cache_ref.at[layer_base + pt_ref[b, j]],
            buf_t.at[:, :, pl.ds(pl.multiple_of(j * PAGE_SIZE, PAGE_SIZE), PAGE_SIZE)], sem.at[j])

    def start(j, c):
        page_copy(j).start()
        return c

    lax.fori_loop(0, n_pages, start, 0)

    new_nsa = p_ref[0, :, C_KVB - C_QB:C_KVB - C_QB + 4 * LANES]
    zpad = jnp.zeros((LANES - nq, LANES), F32)
    for c in (2, 3):
        buf_t[c, :, past:tk] = jnp.concatenate([new_nsa[:, c * LANES:(c + 1) * LANES], zpad], axis=0).T
    wbuf[0:wlen, :] = win_ref[0]
    wbuf[wlen:wlen + nq, :] = p_ref[0, :, C_KVB - C_QB + 4 * LANES:C_KVB - C_QB + 6 * LANES]
    wbuf[wlen + nq:wpad, :] = jnp.zeros((wpad - wlen - nq, 2 * LANES), F32)

    q = p_ref[0, :, 0:W_NSA] * QK_SCALE
    sig = jax.nn.sigmoid(p_ref[0, :, C_GATE - C_QB:C_GATE - C_QB + LANES])
    q_all = jnp.concatenate([_group_queries(q, g, nq) for g in range(G_NSA)], axis=0)
    q_bf = _bf(q_all)

    kk = lax.broadcasted_iota(jnp.int32, (all_rows, wpad), 1)
    dist_w = wlen + (lax.broadcasted_iota(jnp.int32, (all_rows, wpad), 0) % nq) - kk
    mask_w = (dist_w >= 0) & (dist_w < WINDOW) & (kk < wlen + nq)
    p_w = _masked_softmax(_dot_t(q_bf, _bf(wbuf[:, 0:LANES])) + bwin_ref[...], mask_w)
    o_w = _dot(_bf(p_w), _bf(wbuf[:, LANES:2 * LANES]))

    group = max(n_pages // 4, 1)
    for j in range(n_pages):
        if j % group == 0:
            for jj in range(j, min(j + group, n_pages)):
                page_copy(jj).wait()
        for c in range(2):
            rowbuf[c, j * PAGE_SIZE:(j + 1) * PAGE_SIZE, :] = buf_t[c, :, j * PAGE_SIZE:(j + 1) * PAGE_SIZE].T
    kc, vc = _compress_rows((rowbuf.at[0], rowbuf.at[1]), ncp, w1_ref, pos_ref, w2_ref, tail_ref)

    n_idx = lax.broadcasted_iota(jnp.int32, (all_rows, ncp), 1)
    p_c = _masked_softmax(_dot3_t(q_all, kc) + bcmp_ref[...], n_idx < ncp - 1)
    o_c = _dot(_bf(p_c), _bf(vc))
    p_sum = []
    for g in range(G_NSA):
        acc = p_c[g * rows:g * rows + nq]
        for r in range(1, R_NSA):
            acc = acc + p_c[g * rows + r * nq:g * rows + (r + 1) * nq]
        p_sum.append(acc)
    p_hi, p_lo = _split(jnp.concatenate(p_sum + [jnp.zeros((LANES - G_NSA * nq, ncp), F32)], axis=0))
    imp_t = _dot_t(map_ref[...], p_hi) + _dot_t(map_ref[...], p_lo)
    q_pos_t = past + lax.broadcasted_iota(jnp.int32, (nbp, LANES), 1) % nq
    sel = _select_blocks(imp_t, q_pos_t, n_blk, axis=0).T
    sel_rows = jnp.concatenate([sel[g * nq:(g + 1) * nq] for g in range(G_NSA) for _ in range(R_NSA)], axis=0)

    tok = lax.broadcasted_iota(jnp.int32, (all_rows, tk), 1)
    qi_tok = lax.broadcasted_iota(jnp.int32, (all_rows, tk), 0) % nq
    blk_row = lax.broadcasted_iota(jnp.int32, (nbp, tk), 0)
    blk_tok = lax.broadcasted_iota(jnp.int32, (nbp, tk), 1) // SEL_BLOCK
    expand = jnp.where(blk_row == blk_tok, 1.0, 0.0).astype(BF16)
    mk = jnp.logical_and(_dot(_bf(sel_rows), expand) > 0.5, tok <= past + qi_tok)
    p_s = _masked_softmax(_dot(q_bf, _bf(buf_t[2])) + bsel_ref[...], mk)
    o_s = _dot_t(_bf(p_s), _bf(buf_t[3]))

    per_group = lambda o: [o[g * rows:(g + 1) * rows] for g in range(G_NSA)]
    o_ref[0] = _gate_combine(sig, per_group(o_c), per_group(o_s), per_group(o_w), nq)


def _nsa_sample(proj_s, win_cache, cache_pages, page_table, w1bd, posrow, w2bd, sel_map,
                bsel, bcmp, bwin, layer, nq):
    dbsz, n_pages = page_table.shape
    n_pool = cache_pages.shape[0] // (win_cache.shape[0] // dbsz)
    past = n_pages * PAGE_SIZE
    tk = past + LANES
    ncp = past // CMP_STRIDE
    wlen = win_cache.shape[1]
    wpad = wlen + LANES
    n_blk = -(-(past + nq) // SEL_BLOCK)
    kern = functools.partial(_nsa_sample_kernel, layer_base=layer * n_pool, n_pages=n_pages,
                             nq=nq, n_blk=n_blk)
    const2 = lambda b, pt: (0, 0)
    grid_spec = pltpu.PrefetchScalarGridSpec(
        num_scalar_prefetch=1,
        grid=(dbsz,),
        in_specs=[pl.BlockSpec((1, nq, N_PROJ - C_QB), lambda b, pt: (0, b, 1)),
                  pl.BlockSpec((1, wlen, 2 * LANES), lambda b, pt: (layer * dbsz + b, 0, 0)),
                  pl.BlockSpec(memory_space=pl.ANY),
                  pl.BlockSpec((CMP_STRIDE // 2, 2, 2 * LANES, 2 * LANES), lambda b, pt: (0, 0, 0, 0)),
                  pl.BlockSpec((2, 1, 2 * LANES), lambda b, pt: (0, 0, 0)),
                  pl.BlockSpec((2, LANES, LANES), lambda b, pt: (0, 0, 0)),
                  pl.BlockSpec(sel_map.shape, const2),
                  pl.BlockSpec(bsel.shape, const2),
                  pl.BlockSpec(bcmp.shape, const2),
                  pl.BlockSpec(bwin.shape, const2)],
        out_specs=pl.BlockSpec((1, nq, W_NSA), lambda b, pt: (0, b, 0)),
        scratch_shapes=[pltpu.VMEM((4, LANES, tk), F32),
                        pltpu.SemaphoreType.DMA((n_pages,)),
                        pltpu.VMEM((2, past, LANES), F32),
                        pltpu.VMEM((2, ncp + SUBLANES, LANES), F32),
                        pltpu.VMEM((wpad, 2 * LANES), F32)],
    )
    return pl.pallas_call(
        kern, grid_spec=grid_spec,
        out_shape=jax.ShapeDtypeStruct((1, dbsz * nq, W_NSA), F32),
        compiler_params=_cparams(("arbitrary",)),
    )(page_table, proj_s, win_cache, cache_pages, w1bd, posrow, w2bd, sel_map, bsel, bcmp, bwin)


def _out_kernel(y_ref, oa_ref, ob_ref, za_ref, zb_ref, gate_ref, norm_ref, w_ref, g_ref, b_ref, o_ref,
                *, alpha):
    def gated_norm(o, z, nrm):
        ms = jnp.mean(o * o, axis=-1, keepdims=True)
        return o * lax.rsqrt(ms + RMS_EPS) * nrm * (z * jax.nn.sigmoid(z))

    ua = gated_norm(oa_ref[0], za_ref[0], norm_ref[:, 0:W_SB])
    ub = gated_norm(ob_ref[0], zb_ref[0], norm_ref[:, W_SB:D_MODEL])
    u = _bf(jnp.concatenate([ua, ub], axis=1))
    mixed = _dot(u, w_ref[...])
    x = alpha * y_ref[0] + gate_ref[0] * mixed
    mu = jnp.mean(x, axis=-1, keepdims=True)
    xc = x - mu
    var = jnp.mean(xc * xc, axis=-1, keepdims=True)
    o_ref[0] = xc * lax.rsqrt(var + LN_EPS) * g_ref[...] + b_ref[...]


def _out_stage(y, o_a, o_b, proj, gate, norm_p, w_out_bf, ln_g, ln_b, alpha):
    bsz, t, d = y.shape
    tm = min(256, t)
    if gate.shape[1] == 1:
        gate_spec = pl.BlockSpec((1, 1, d), lambda b, i: (b, 0, 0))
    else:
        gate_spec = pl.BlockSpec((1, tm, d), lambda b, i: (b, i, 0))
    row = lambda w: pl.BlockSpec((1, w), lambda b, i: (0, 0))
    return pl.pallas_call(
        functools.partial(_out_kernel, alpha=alpha),
        grid=(bsz, t // tm),
        in_specs=[pl.BlockSpec((1, tm, d), lambda b, i: (b, i, 0)),
                  pl.BlockSpec((1, tm, W_SB), lambda b, i: (b, i, 0)),
                  pl.BlockSpec((1, tm, W_NSA), lambda b, i: (b, i, 0)),
                  pl.BlockSpec((1, tm, W_SB), lambda b, i: (b, i, C_ZA // W_SB)),
                  pl.BlockSpec((1, tm, W_NSA), lambda b, i: (b, i, C_ZB // W_NSA)),
                  gate_spec, row(d),
                  pl.BlockSpec((d, d), lambda b, i: (0, 0)),
                  row(d), row(d)],
        out_specs=pl.BlockSpec((1, tm, d), lambda b, i: (b, i, 0)),
        out_shape=jax.ShapeDtypeStruct((bsz, t, d), F32),
        compiler_params=_cparams(("parallel", "parallel")),
    )(y, o_a, o_b, proj, proj, gate, norm_p.reshape(1, d), w_out_bf, ln_g.reshape(1, d), ln_b.reshape(1, d))


def _rel_bucket(dist):
    n = jnp.maximum(dist, 0)
    nf = jnp.maximum(n, 1).astype(F32)
    large = REL_MAX_EXACT + (jnp.log(nf / REL_MAX_EXACT) / math.log(REL_MAX_DIST / REL_MAX_EXACT)
                             * (N_BUCKETS - REL_MAX_EXACT)).astype(jnp.int32)
    large = jnp.minimum(large, N_BUCKETS - 1)
    return jnp.where(n < REL_MAX_EXACT, n, large)


def _nsa_perm(x, axis):
    shp = x.shape
    x = x.reshape(shp[:axis] + (G_NSA, R_NSA, HEAD_DIM) + shp[axis + 1:])
    x = jnp.swapaxes(x, axis, axis + 1)
    return x.reshape(shp)


def _bias_tile_kernel(base_ref, step_ref, thr_ref, rb_ref, o_ref):
    t = pl.program_id(0)
    shape = o_ref.shape[2:]
    dist = (base_ref[t] + lax.broadcasted_iota(jnp.int32, shape, 0)
            - step_ref[t] * lax.broadcasted_iota(jnp.int32, shape, 1))
    for h in range(H_NSA):
        val = jnp.full(shape, rb_ref[0, h], F32)
        for b in range(1, N_BUCKETS):
            val = jnp.where(dist >= thr_ref[b], rb_ref[b, h], val)
        o_ref[0, h] = val


def _bias_tiles(base, step, thr, rel_bias, rows):
    n = len(base)
    grid_spec = pltpu.PrefetchScalarGridSpec(
        num_scalar_prefetch=3,
        grid=(n,),
        in_specs=[pl.BlockSpec(memory_space=pltpu.SMEM)],
        out_specs=pl.BlockSpec((1, H_NSA, rows, LANES), lambda t, *_: (t, 0, 0, 0)),
    )
    return pl.pallas_call(
        _bias_tile_kernel, grid_spec=grid_spec,
        out_shape=jax.ShapeDtypeStruct((n, H_NSA, rows, LANES), F32),
        compiler_params=_cparams(("parallel",)),
    )(jnp.asarray(base, jnp.int32), jnp.asarray(step, jnp.int32), thr, rel_bias)


def _bias_tables(rel_bias, nq, past, wlen):
    bucket = _rel_bucket(jnp.arange(BIAS_DMAX))
    thr = jnp.sum(bucket[None, :] < jnp.arange(N_BUCKETS)[:, None], axis=1).astype(jnp.int32)
    base = [LANES * d for d in range(N_TOK_TILES)] + [LANES * o - (CMP_LEN - 1) for o in range(N_CMP_TILES)]
    step = [1] * N_TOK_TILES + [CMP_STRIDE] * N_CMP_TILES
    tiles = _bias_tiles(base, step, thr, rel_bias, TQ)
    tt, tc = tiles[:N_TOK_TILES], tiles[N_TOK_TILES:]
    n_key, n_cmp, n_win = past // LANES + 1, past // (CMP_STRIDE * LANES), wlen // LANES + 1
    base = ([past - LANES * k for k in range(n_key)]
            + [past - CMP_STRIDE * LANES * j - (CMP_LEN - 1) for j in range(n_cmp)]
            + [wlen - LANES * k for k in range(n_win)])
    step = [1] * n_key + [CMP_STRIDE] * n_cmp + [1] * n_win
    tiles = _bias_tiles(base, step, thr, rel_bias, nq)
    rows = lambda x: jnp.transpose(x, (1, 2, 0, 3)).reshape(H_NSA * nq, x.shape[0] * LANES)
    bsel, bcmp, bwin = rows(tiles[:n_key]), rows(tiles[n_key:n_key + n_cmp]), rows(tiles[n_key + n_cmp:])
    return tt, tc, bsel, bcmp, bwin


def _selection_map(ncp, n_c, n_blk):
    nbp = -(-n_blk // LANES) * LANES
    cs = np.arange(ncp)[:, None] * CMP_STRIDE
    ss = np.arange(nbp)[None, :] * SEL_BLOCK
    ov = np.minimum(cs + CMP_LEN, ss + SEL_BLOCK) - np.maximum(cs, ss)
    m = np.clip(ov, 0, None).astype(np.float32) / CMP_LEN
    m = m * (np.arange(ncp)[:, None] < n_c) * (np.arange(nbp)[None, :] < n_blk)
    return jnp.asarray(m, dtype=BF16)


def _block_diag2(w):
    z = jnp.zeros_like(w)
    return jnp.concatenate([jnp.concatenate([w, z], axis=-1), jnp.concatenate([z, w], axis=-1)], axis=-2)


def kernel(x_prompt, x_sample, cache_sb_kv, cache_nsa_kv, cache_win_kv, page_table, c_prompt, c_sample,
           ln_in_g, ln_in_b, w_ada, b_ada, w_in, w_cmp1, w_cmp2, pos_cmp, norm_grp, w_out, ln_g, ln_b,
           rel_bias):
    bsz, seq, d = x_prompt.shape
    dbsz, nq, _ = x_sample.shape
    depth = w_ada.shape[0]
    n_pool = cache_sb_kv.shape[1]
    n_pages = page_table.shape[1]
    past = n_pages * PAGE_SIZE
    wlen = cache_win_kv.shape[2]
    alpha = (2 * depth) ** 0.25
    assert d == D_MODEL and seq % (16 * TQ) == 0 and past % (16 * TQ) == 0 and wlen == WINDOW
    assert seq // SEL_BLOCK >= SEL_TOPK and nq == SUBLANES

    w_qb = _nsa_perm(w_in[:, :, 2048:2560], 2)
    w_zb = _nsa_perm(w_in[:, :, 3352:3864], 2)
    pad = lambda n: jnp.zeros((depth, d, n), F32)
    w_in_p = _bf(jnp.concatenate(
        [w_in[:, :, 0:2048], w_qb, w_in[:, :, 2560:3328], w_in[:, :, 3328:3352], pad(C_ZB - C_GATE - 3 * H_NSA),
         w_zb], axis=2))
    norm_p = jnp.concatenate([norm_grp[:, :W_SB], _nsa_perm(norm_grp[:, W_SB:], 1)], axis=1)
    w_out_p = _bf(jnp.concatenate([w_out[:, :W_SB], _nsa_perm(w_out[:, W_SB:], 1)], axis=1))
    w1b = jnp.swapaxes(_block_diag2(w_cmp1), 1, 2)

    def pair_rows(w):
        w = jnp.swapaxes(w.reshape(depth, CMP_STRIDE // 2, 2, 2, LANES, LANES), 2, 3)
        return w.reshape(depth, CMP_STRIDE // 2, 2, 2 * LANES, LANES)

    w1bd = _bf(jnp.concatenate([pair_rows(w1b[:, :CMP_STRIDE]), pair_rows(w1b[:, CMP_STRIDE:])], axis=-1))
    w2bd = _bf(_block_diag2(w_cmp2))
    pos2 = jnp.einsum('dcxlk,dcxlkh->dcxh', pos_cmp.reshape(depth, 2, 2, CMP_STRIDE, HEAD_DIM),
                      w_cmp1.reshape(depth, 2, 2, CMP_STRIDE, HEAD_DIM, HEAD_DIM),
                      precision=lax.Precision.HIGHEST)
    posrow = jnp.concatenate([pos2[:, :, 0], pos2[:, :, 0], pos2[:, :, 1], pos2[:, :, 1]],
                             axis=-1)[:, :, None, :]
    tt, tc, bsel, bcmp, bwin = _bias_tables(rel_bias, nq, past, wlen)
    tts = (tt - tt[N_TOK_TILES - 1:]) * LOG2E
    map_p = _selection_map(seq // CMP_STRIDE, (seq - CMP_LEN) // CMP_STRIDE + 1, seq // SEL_BLOCK)
    n_blk_s = -(-(past + nq) // SEL_BLOCK)
    map_s = _selection_map(past // CMP_STRIDE, (past + nq - CMP_LEN) // CMP_STRIDE + 1, n_blk_s).T

    n_c = bsz + dbsz
    c_rows = -(-n_c // SUBLANES) * SUBLANES
    c_all = jnp.concatenate([c_prompt, c_sample, jnp.zeros((c_rows - n_c, d), F32)], axis=0)
    mod = _ada_mod(c_all, w_ada, b_ada)

    y_p = _layer_norm(x_prompt.reshape(bsz * seq, d), ln_in_g, ln_in_b).reshape(bsz, seq, d)
    y_s = _layer_norm(x_sample.reshape(dbsz * nq, d), ln_in_g, ln_in_b).reshape(1, dbsz * nq, d)

    sb_pages = jnp.transpose(cache_sb_kv, (0, 1, 3, 4, 5, 2)).reshape(depth * n_pool, 2, H_SB, HEAD_DIM, PAGE_SIZE)
    nsa_pages = jnp.transpose(cache_nsa_kv, (0, 1, 3, 4, 5, 2)).reshape(depth * n_pool, 4, KV_NSA, PAGE_SIZE)
    win_cache = cache_win_kv.reshape(depth * dbsz, wlen, 2 * KV_NSA)

    p_sb, p_nsa, p_win, s_sb, s_nsa, s_win = [], [], [], [], [], []
    for l in range(depth):
        mp = mod[l, :bsz].reshape(bsz, 1, 3 * d)
        proj = _in_proj(y_p, mp[:, :, 0:d], mp[:, :, d:2 * d], w_in_p[l])
        o_a = _sb_prompt(proj)
        kcvc = _compress_prompt(proj, w1bd[l], posrow[l], w2bd[l])
        o_b = _nsa_prompt(proj, kcvc, tt, tts, tc, map_p)
        p_sb.append(proj[:, :, C_KA:C_ZA].reshape(bsz, seq, 2, H_SB, HEAD_DIM))
        p_nsa.append(proj[:, :, C_KVB:C_KVB + 4 * KV_NSA].reshape(bsz, seq, 4, G_NSA, HEAD_DIM))
        p_win.append(proj[:, seq - min(WINDOW, seq):, C_KVB + 4 * KV_NSA:C_KVB + 6 * KV_NSA]
                     .reshape(bsz, min(WINDOW, seq), 2, G_NSA, HEAD_DIM))
        y_p = _out_stage(y_p, o_a, o_b, proj, mp[:, :, 2 * d:3 * d], norm_p[l], w_out_p[l], ln_g[l], ln_b[l],
                         alpha)
        ms = jnp.repeat(mod[l, bsz:bsz + dbsz], nq, axis=0).reshape(1, dbsz * nq, 3 * d)
        proj_s = _in_proj(y_s, ms[:, :, 0:d], ms[:, :, d:2 * d], w_in_p[l])
        o_a = _sb_sample(proj_s, sb_pages, page_table, l * n_pool, nq)
        o_b = _nsa_sample(proj_s, win_cache, nsa_pages, page_table, w1bd[l], posrow[l], w2bd[l], map_s,
                          bsel, bcmp, bwin, l, nq)
        ps = proj_s.reshape(dbsz, nq, N_PROJ)
        s_sb.append(ps[:, :, C_KA:C_ZA].reshape(dbsz, nq, 2, H_SB, HEAD_DIM))
        s_nsa.append(ps[:, :, C_KVB:C_KVB + 4 * KV_NSA].reshape(dbsz, nq, 4, G_NSA, HEAD_DIM))
        new_win = ps[:, :, C_KVB + 4 * KV_NSA:C_KVB + 6 * KV_NSA].reshape(dbsz, nq, 2, G_NSA, HEAD_DIM)
        win_all = jnp.concatenate([cache_win_kv[l], new_win], axis=1)
        s_win.append(win_all[:, win_all.shape[1] - min(WINDOW, past + nq):])
        y_s = _out_stage(y_s, o_a, o_b, proj_s, ms[:, :, 2 * d:3 * d], norm_p[l], w_out_p[l], ln_g[l],
                         ln_b[l], alpha)

    return (y_p, y_s.reshape(dbsz, nq, d), jnp.stack(p_sb), jnp.stack(p_nsa), jnp.stack(p_win),
            jnp.stack(s_sb), jnp.stack(s_nsa), jnp.stack(s_win))
```

```python
import functools
import math

import numpy as np
import jax
import jax.numpy as jnp
from jax import lax
from jax.experimental import pallas as pl
from jax.experimental.pallas import tpu as pltpu

F32 = jnp.float32
BF16 = jnp.bfloat16

D_MODEL = 1024
HEAD_DIM = 64
W_SB = D_MODEL // 2
W_NSA = D_MODEL - W_SB
H_SB = W_SB // HEAD_DIM
H_NSA = W_NSA // HEAD_DIM
G_NSA = 2
R_NSA = H_NSA // G_NSA
KV_NSA = G_NSA * HEAD_DIM
CMP_LEN = 32
CMP_STRIDE = 16
SEL_BLOCK = 64
SEL_TOPK = 16
WINDOW = 512
PAGE_SIZE = 128
N_BUCKETS = 32
REL_MAX_EXACT = N_BUCKETS // 2
REL_MAX_DIST = 1024
LN_EPS = 1e-5
RMS_EPS = 1e-6
NEG = -1e30
MASK_VALUE = -1e30
FORCE_SCORE = 1e4
QK_SCALE = HEAD_DIM ** -0.5
LOG2E = math.log2(math.e)

LANES = 128
SUBLANES = 8
VMEM_LIMIT = 56 * 1024 * 1024

C_QA, C_KA, C_VA, C_ZA, C_QB, C_KVB = 0, 512, 1024, 1536, 2048, 2560
C_GATE, C_ZB, N_PROJ = 3328, 3584, 4096
SB_DEAD = -104.0
BIAS_DMAX = 1024
N_TOK_TILES = 9
N_CMP_TILES = 24
TQ = 128


def _bf(x):
    return x.astype(BF16)


def _dot(a, b):
    return jnp.dot(a, b, preferred_element_type=F32)


def _dot_t(a, b):
    return lax.dot_general(a, b, (((1,), (1,)), ((), ())), preferred_element_type=F32)


def _split(x):
    hi = _bf(x)
    lo = _bf(x - hi.astype(F32))
    return hi, lo


def _dot3_t(a, b):
    ah, al = _split(a)
    bh, bl = _split(b)
    return _dot_t(ah, bh) + _dot_t(ah, bl) + _dot_t(al, bh)


def _dot3(a, b):
    ah, al = _split(a)
    bh, bl = _split(b)
    return _dot(ah, bh) + _dot(ah, bl) + _dot(al, bh)


def _dot2_exact_rhs(a, b_bf):
    ah, al = _split(a)
    return _dot(ah, b_bf) + _dot(al, b_bf)


def _softplus(z):
    return jnp.maximum(z, 0.0) + jnp.log1p(jnp.exp(-jnp.abs(z)))


def _cparams(sem):
    return pltpu.CompilerParams(dimension_semantics=sem, vmem_limit_bytes=VMEM_LIMIT)


def _ln_kernel(x_ref, g_ref, b_ref, o_ref):
    x = x_ref[...]
    mu = jnp.mean(x, axis=-1, keepdims=True)
    xc = x - mu
    var = jnp.mean(xc * xc, axis=-1, keepdims=True)
    o_ref[...] = xc * lax.rsqrt(var + LN_EPS) * g_ref[...] + b_ref[...]


def _layer_norm(x2d, g, b):
    rows, d = x2d.shape
    tm = min(512, rows)
    return pl.pallas_call(
        _ln_kernel,
        grid=(rows // tm,),
        in_specs=[pl.BlockSpec((tm, d), lambda i: (i, 0)),
                  pl.BlockSpec((1, d), lambda i: (0, 0)),
                  pl.BlockSpec((1, d), lambda i: (0, 0))],
        out_specs=pl.BlockSpec((tm, d), lambda i: (i, 0)),
        out_shape=jax.ShapeDtypeStruct((rows, d), F32),
        compiler_params=_cparams(("parallel",)),
    )(x2d, g.reshape(1, d), b.reshape(1, d))


def _mod_kernel(c_ref, w_ref, b_ref, o_ref):
    c = c_ref[...]
    s = c * jax.nn.sigmoid(c)
    o_ref[0] = _dot3(s, w_ref[0]) + b_ref[0]


def _ada_mod(c_all, w_ada, b_ada):
    depth, d, n3 = w_ada.shape
    rows = c_all.shape[0]
    tn = 1024
    return pl.pallas_call(
        _mod_kernel,
        grid=(depth, n3 // tn),
        in_specs=[pl.BlockSpec((rows, d), lambda l, j: (0, 0)),
                  pl.BlockSpec((1, d, tn), lambda l, j: (l, 0, j)),
                  pl.BlockSpec((1, 1, tn), lambda l, j: (l, 0, j))],
        out_specs=pl.BlockSpec((1, rows, tn), lambda l, j: (l, 0, j)),
        out_shape=jax.ShapeDtypeStruct((depth, rows, n3), F32),
        compiler_params=_cparams(("parallel", "parallel")),
    )(c_all, w_ada, b_ada.reshape(depth, 1, n3))


def _inproj_kernel(y_ref, sh_ref, sc_ref, w_ref, o_ref):
    h = y_ref[0] * (1.0 + sc_ref[0]) + sh_ref[0]
    hb = _bf(h)
    chunk = 512
    for c0 in range(0, N_PROJ, chunk):
        o_ref[0, :, c0:c0 + chunk] = _dot(hb, w_ref[:, c0:c0 + chunk])


def _in_proj(y, shift, scale, w_bf):
    bsz, t, d = y.shape
    tm = min(256, t)
    tmod = shift.shape[1]
    if tmod == 1:
        mod_spec = pl.BlockSpec((1, 1, d), lambda b, i: (b, 0, 0))
    else:
        mod_spec = pl.BlockSpec((1, tm, d), lambda b, i: (b, i, 0))
    return pl.pallas_call(
        _inproj_kernel,
        grid=(bsz, t // tm),
        in_specs=[pl.BlockSpec((1, tm, d), lambda b, i: (b, i, 0)), mod_spec, mod_spec,
                  pl.BlockSpec((d, N_PROJ), lambda b, i: (0, 0))],
        out_specs=pl.BlockSpec((1, tm, N_PROJ), lambda b, i: (b, i, 0)),
        out_shape=jax.ShapeDtypeStruct((bsz, t, N_PROJ), F32),
        compiler_params=_cparams(("parallel", "parallel")),
    )(y, shift, scale, w_bf)


def _sb_tile(qh, kt, vt, mask, carry, tri):
    a, carry = _sb_weights(_dot_t(qh, kt), mask, carry, tri)
    return _dot(_bf(a), vt), carry


def _sb_weights(z, mask, carry, tri):
    lk = -_softplus(z)
    if mask is not None:
        lk = jnp.where(mask, lk, 0.0)
    after = []
    for blk in reversed(range(z.shape[1] // LANES)):
        lk_b = lk[:, blk * LANES:(blk + 1) * LANES]
        hi, lo = _split(lk_b)
        after.insert(0, _dot(hi, tri) + _dot(lo, tri) + carry)
        carry = carry + jnp.sum(lk_b, axis=1, keepdims=True)
    after = jnp.concatenate(after, axis=1) if len(after) > 1 else after[0]
    a = jnp.exp(z + lk + after)
    if mask is not None:
        a = jnp.where(mask, a, 0.0)
    return a, carry


def _tri_newer():
    r = lax.broadcasted_iota(jnp.int32, (LANES, LANES), 0)
    c = lax.broadcasted_iota(jnp.int32, (LANES, LANES), 1)
    return jnp.where(r > c, 1.0, 0.0).astype(BF16)


def _sb_prompt_kernel(q_ref, k_ref, v_ref, o_ref, *, tq):
    i = pl.program_id(2)
    q0 = i * tq
    q = q_ref[0] * QK_SCALE
    lane = lax.broadcasted_iota(jnp.int32, (tq, LANES), 1)
    row_pos = q0 + lax.broadcasted_iota(jnp.int32, (tq, LANES), 0)
    tri = _tri_newer()
    qh = _bf(jnp.concatenate([jnp.where(lane < HEAD_DIM, q, 0.0), jnp.where(lane >= HEAD_DIM, q, 0.0)], axis=0))
    def chunk(c, mask, carry):
        k0 = pl.multiple_of(c * tq, tq)
        kt = _bf(k_ref[0, pl.ds(k0, tq), :])
        vt = _bf(v_ref[0, pl.ds(k0, tq), :])
        return _sb_tile(qh, kt, vt, mask, carry, tri)

    key_pos = q0 + lax.broadcasted_iota(jnp.int32, (2 * tq, tq), 1)
    qry_pos = q0 + lax.broadcasted_iota(jnp.int32, (2 * tq, tq), 0) % tq
    acc, carry = chunk(i, key_pos < qry_pos, jnp.zeros((2 * tq, LANES), F32))

    def cond(s):
        c, alive, _, _ = s
        return jnp.logical_and(c >= 0, alive)

    def body(s):
        c, _, carry, acc = s
        contrib, carry = chunk(c, None, carry)
        return c - 1, jnp.max(carry) > SB_DEAD, carry, acc + contrib

    _, _, _, acc = lax.while_loop(cond, body, (i - 1, jnp.max(carry) > SB_DEAD, carry, acc))
    o_ref[0] = jnp.where(lane < HEAD_DIM, acc[0:tq], acc[tq:2 * tq])


def _sb_prompt(proj, tq=256):
    bsz, t, _ = proj.shape
    tq = min(tq, t)
    npair = W_SB // LANES
    return pl.pallas_call(
        functools.partial(_sb_prompt_kernel, tq=tq),
        grid=(bsz, npair, t // tq),
        in_specs=[pl.BlockSpec((1, tq, LANES), lambda b, p, i: (b, i, C_QA // LANES + p)),
                  pl.BlockSpec((1, t, LANES), lambda b, p, i: (b, 0, C_KA // LANES + p)),
                  pl.BlockSpec((1, t, LANES), lambda b, p, i: (b, 0, C_VA // LANES + p))],
        out_specs=pl.BlockSpec((1, tq, LANES), lambda b, p, i: (b, i, p)),
        out_shape=jax.ShapeDtypeStruct((bsz, t, W_SB), F32),
        compiler_params=_cparams(("parallel", "parallel", "arbitrary")),
    )(proj, proj, proj)


def _sb_sample_kernel(pt_ref, p_ref, last_ref, cache_ref, o_ref, buf, sem, carry_ref, acc_ref,
                      *, layer_base, n_pages, nq):
    b = pl.program_id(0)
    rows = H_SB * nq
    q = p_ref[0, :, C_QA:C_QA + W_SB] * QK_SCALE
    lane_c = lax.broadcasted_iota(jnp.int32, (rows, W_SB), 1)
    row_c = lax.broadcasted_iota(jnp.int32, (rows, W_SB), 0)
    own = (lane_c // HEAD_DIM) == (row_c // nq)
    qbd = _bf(jnp.where(own, jnp.concatenate([q] * H_SB, axis=0), 0.0))
    tri = _tri_newer()
    key = lax.broadcasted_iota(jnp.int32, (rows, LANES), 1)
    qi = lax.broadcasted_iota(jnp.int32, (rows, LANES), 0) % nq

    pad = jnp.zeros((LANES - nq, W_SB), F32)
    k_new = _bf(jnp.concatenate([p_ref[0, :, C_KA:C_KA + W_SB], pad], axis=0))
    v_new = _bf(jnp.concatenate([p_ref[0, :, C_VA:C_VA + W_SB], pad], axis=0))
    contrib, carry = _sb_tile(qbd, k_new, v_new, key < qi, jnp.zeros((rows, LANES), F32), tri)
    carry_ref[...] = carry
    acc_ref[...] = contrib

    def page_tile(kv_ref):
        kt_t = _bf(kv_ref[0].reshape(W_SB, PAGE_SIZE))
        vt_t = _bf(kv_ref[1].reshape(W_SB, PAGE_SIZE))
        a, carry = _sb_weights(_dot(qbd, kt_t), None, carry_ref[...], tri)
        acc_ref[...] += _dot_t(_bf(a), vt_t)
        carry_ref[...] = carry

    @pl.when(jnp.max(carry_ref[...]) > SB_DEAD)
    def _():
        page_tile(last_ref.at[0])

    def cond(c):
        j, alive = c
        return jnp.logical_and(j >= 0, alive)

    def body(c):
        j, _ = c
        page = layer_base + pt_ref[b, j]
        cp = pltpu.make_async_copy(cache_ref.at[page], buf, sem)
        cp.start()
        cp.wait()
        page_tile(buf)
        return j - 1, jnp.max(carry_ref[...]) > SB_DEAD

    lax.while_loop(cond, body, (n_pages - 2, jnp.max(carry_ref[...]) > SB_DEAD))

    acc = jnp.where(own, acc_ref[...], 0.0)
    o = acc[0:nq]
    for h in range(1, H_SB):
        o = o + acc[h * nq:(h + 1) * nq]
    o_ref[0] = o


def _sb_sample(proj_s, cache_pages, page_table, layer_base, nq):
    dbsz, n_pages = page_table.shape
    rows = H_SB * nq
    page_shape = cache_pages.shape[1:]
    kern = functools.partial(_sb_sample_kernel, layer_base=layer_base, n_pages=n_pages, nq=nq)
    grid_spec = pltpu.PrefetchScalarGridSpec(
        num_scalar_prefetch=1,
        grid=(dbsz,),
        in_specs=[pl.BlockSpec((1, nq, C_QB), lambda b, pt: (0, b, 0)),
                  pl.BlockSpec((1,) + page_shape,
                               lambda b, pt: (layer_base + pt[b, n_pages - 1], 0, 0, 0, 0)),
                  pl.BlockSpec(memory_space=pl.ANY)],
        out_specs=pl.BlockSpec((1, nq, W_SB), lambda b, pt: (0, b, 0)),
        scratch_shapes=[pltpu.VMEM(page_shape, F32),
                        pltpu.SemaphoreType.DMA(()),
                        pltpu.VMEM((rows, LANES), F32),
                        pltpu.VMEM((rows, W_SB), F32)],
    )
    return pl.pallas_call(
        kern, grid_spec=grid_spec,
        out_shape=jax.ShapeDtypeStruct((1, dbsz * nq, W_SB), F32),
        compiler_params=_cparams(("arbitrary",)),
    )(page_table, proj_s, cache_pages, cache_pages)


def _compress_rows(x_refs, ncp, w1_ref, bias_ref, w2_ref, tail_ref):
    outs = []
    for c, x_ref in enumerate(x_refs):
        acc = jnp.zeros((ncp, 2 * LANES), F32)
        for m in range(CMP_STRIDE // 2):
            r = jnp.concatenate([x_ref[pl.ds(2 * m, ncp, stride=CMP_STRIDE), :],
                                 x_ref[pl.ds(2 * m + 1, ncp, stride=CMP_STRIDE), :]], axis=1)
            acc = acc + _dot(_bf(r), w1_ref[m, c])
        acc = acc + bias_ref[c]
        tail_ref[c, 0:ncp, :] = acc[:, LANES:2 * LANES]
        tail_ref[c, ncp:ncp + SUBLANES, :] = jnp.zeros((SUBLANES, LANES), F32)
        hid = jax.nn.gelu(acc[:, 0:LANES] + tail_ref[c, pl.ds(1, ncp), :])
        outs.append(_dot(_bf(hid), w2_ref[c]))
    return outs


def _compress_kernel(xk_ref, xv_ref, w1_ref, pos_ref, w2_ref, o_ref, tail_ref, *, ncp):
    kc, vc = _compress_rows((xk_ref.at[0], xv_ref.at[0]), ncp, w1_ref, pos_ref, w2_ref, tail_ref)
    o_ref[0, :, 0:LANES] = kc
    o_ref[0, :, LANES:2 * LANES] = vc


def _compress_prompt(proj, w1bd, posrow, w2bd):
    bsz, t, _ = proj.shape
    ncp = t // CMP_STRIDE
    return pl.pallas_call(
        functools.partial(_compress_kernel, ncp=ncp),
        grid=(bsz,),
        in_specs=[pl.BlockSpec((1, t, LANES), lambda b: (b, 0, C_KVB // LANES)),
                  pl.BlockSpec((1, t, LANES), lambda b: (b, 0, C_KVB // LANES + 1)),
                  pl.BlockSpec((CMP_STRIDE // 2, 2, 2 * LANES, 2 * LANES), lambda b: (0, 0, 0, 0)),
                  pl.BlockSpec((2, 1, 2 * LANES), lambda b: (0, 0, 0)),
                  pl.BlockSpec((2, LANES, LANES), lambda b: (0, 0, 0))],
        out_specs=pl.BlockSpec((1, ncp, 2 * LANES), lambda b: (b, 0, 0)),
        out_shape=jax.ShapeDtypeStruct((bsz, ncp, 2 * LANES), F32),
        scratch_shapes=[pltpu.VMEM((2, ncp + SUBLANES, LANES), F32)],
        compiler_params=_cparams(("parallel",)),
    )(proj, proj, w1bd, posrow, w2bd)


def _group_queries(q, g, nq):
    lane = lax.broadcasted_iota(jnp.int32, (nq, LANES), 1)
    in_g = (lane >= HEAD_DIM) if g else (lane < HEAD_DIM)
    return jnp.concatenate(
        [jnp.where(in_g, q[:, r * LANES:(r + 1) * LANES], 0.0) for r in range(R_NSA)], axis=0)


def _masked_softmax(s, mask):
    s = jnp.where(mask, s, NEG)
    m = jnp.max(s, axis=1, keepdims=True)
    p = jnp.where(mask, jnp.exp(s - m), 0.0)
    l = jnp.sum(p, axis=1, keepdims=True)
    return p / jnp.maximum(l, 1e-30)


def _select_blocks(imp, q_pos, n_blk, axis):
    nb = imp.shape[axis]
    blk = lax.broadcasted_iota(jnp.int32, imp.shape, axis)
    blk_f = blk.astype(F32)
    cur = q_pos // SEL_BLOCK
    forced = (blk == 0) | (blk == cur) | (blk == cur - 1)
    valid = blk * SEL_BLOCK <= q_pos
    score = jnp.where(forced, FORCE_SCORE, jnp.where(valid, imp, -1.0))
    score = jnp.where(blk < n_blk, score, -jnp.inf)
    sel = jnp.zeros(imp.shape, F32)
    for _ in range(SEL_TOPK):
        m = jnp.max(score, axis=axis, keepdims=True)
        first = jnp.min(jnp.where(score == m, blk_f, float(nb)), axis=axis, keepdims=True)
        pick = blk_f == first
        sel = jnp.where(pick, 1.0, sel)
        score = jnp.where(pick, -jnp.inf, score)
    return sel


def _gate_combine(sig, o_c, o_s, o_w, nq):
    lane = lax.broadcasted_iota(jnp.int32, (nq, LANES), 1)
    slabs = []
    for r in range(R_NSA):
        per_g = []
        for g in range(G_NSA):
            h = g * R_NSA + r
            rows = slice(r * nq, (r + 1) * nq)
            acc = sig[:, h:h + 1] * o_c[g][rows]
            acc = acc + sig[:, H_NSA + h:H_NSA + h + 1] * o_s[g][rows]
            acc = acc + sig[:, 2 * H_NSA + h:2 * H_NSA + h + 1] * o_w[g][rows]
            per_g.append(acc)
        slabs.append(jnp.where(lane < HEAD_DIM, per_g[0], per_g[1]))
    return jnp.concatenate(slabs, axis=1)


def _nsa_prompt_kernel(q_ref, gate_ref, kcvc_ref, selkv_ref, w0_ref, w1_ref, w2_ref, w3_ref, w4_ref,
                       tt_ref, tts_ref, tca_ref, tcb_ref, tcf_ref, mapt_ref, o_ref,
                       qg_ref, qa_ref, m_ref, acc_ref, *, ncp, n_blk):
    tq = TQ
    i = pl.program_id(1)
    q0 = i * tq
    rows = R_NSA * tq
    q = q_ref[0] * QK_SCALE
    lane = lax.broadcasted_iota(jnp.int32, (tq, LANES), 1)
    q_pos = q0 + lax.broadcasted_iota(jnp.int32, (tq, LANES), 0)
    nbp = mapt_ref.shape[0]
    q_pos_t = q0 + lax.broadcasted_iota(jnp.int32, (nbp, tq), 1)
    win_refs = (w0_ref, w1_ref, w2_ref, w3_ref, w4_ref)
    n_ctile = ncp // LANES
    jn_a = i // 16
    o_c, o_s, o_w = [], [], []

    for g in range(G_NSA):
        qg = _group_queries(q, g, tq)
        qg_ref[g] = _bf(qg)
        s = _dot3_t(qg, kcvc_ref[0, :, 0:LANES])
        n_idx = lax.broadcasted_iota(jnp.int32, (tq, ncp), 1)
        qp_c = q0 + lax.broadcasted_iota(jnp.int32, (tq, ncp), 0)
        mask_c = (n_idx * CMP_STRIDE + CMP_LEN - 1) <= qp_c
        bias_rows = []
        for r in range(R_NSA):
            h = g * R_NSA + r
            tiles = []
            for jn in range(n_ctile):
                t = jnp.where(jn == jn_a, tca_ref[0, h],
                              jnp.where(jn == jn_a - 1, tcb_ref[0, h], tcf_ref[0, h]))
                tiles.append(t)
            bias_rows.append(jnp.concatenate(tiles, axis=1) if n_ctile > 1 else tiles[0])
        bias_c = jnp.concatenate(bias_rows, axis=0)
        mask_c4 = jnp.concatenate([mask_c] * R_NSA, axis=0)
        p_c = _masked_softmax(s + bias_c, mask_c4)
        o_c.append(_dot(_bf(p_c), _bf(kcvc_ref[0, :, LANES:2 * LANES])))
        p_sum = p_c[0:tq]
        for r in range(1, R_NSA):
            p_sum = p_sum + p_c[r * tq:(r + 1) * tq]
        p_hi, p_lo = _split(p_sum)
        imp_t = _dot_t(mapt_ref[...], p_hi) + _dot_t(mapt_ref[...], p_lo)
        sel_t = _select_blocks(imp_t, q_pos_t, n_blk, axis=0)
        not_sel = _bf(1.0 - sel_t.T)
        qa_ref[g] = jnp.concatenate([_bf(qg * LOG2E), jnp.concatenate([not_sel] * R_NSA, axis=0)], axis=1)

    m_ref[...] = jnp.full((G_NSA, rows, LANES), NEG, F32)
    acc_ref[...] = jnp.zeros((G_NSA, rows, LANES), F32)
    causal4 = jnp.concatenate([(q0 + lane) <= q_pos] * R_NSA, axis=0)

    def sel_chunk(j0, nt, near, causal=False):
        width = nt * LANES
        k0 = pl.multiple_of(j0 * LANES, LANES)
        kv = selkv_ref[0, pl.ds(k0, width), :]
        v = kv[:, LANES:2 * LANES]
        v_lane = lax.broadcasted_iota(jnp.int32, (width, LANES), 1)
        key_blk = 2 * j0 + lax.broadcasted_iota(jnp.int32, (width, nbp), 0) // SEL_BLOCK
        blk_col = lax.broadcasted_iota(jnp.int32, (width, nbp), 1)
        k_aug = jnp.concatenate([_bf(kv[:, 0:LANES]),
                                 jnp.where(key_blk == blk_col, MASK_VALUE, 0.0).astype(BF16)], axis=1)
        scs = [_dot_t(qa_ref[g], k_aug) for g in range(G_NSA)]
        for g in range(G_NSA):
            vt = _bf(jnp.where((v_lane < HEAD_DIM) if g else (v_lane >= HEAD_DIM), 1.0, v))
            sc = scs[g]
            if near:
                bias_t = []
                for t in range(nt):
                    d = jnp.minimum(i - j0 - t, N_TOK_TILES - 1)
                    bias_t.append(jnp.concatenate([tts_ref[d, g * R_NSA + r] for r in range(R_NSA)], axis=0))
                sc = sc + (jnp.concatenate(bias_t, axis=1) if nt > 1 else bias_t[0])
            if causal:
                sc = jnp.where(causal4, sc, NEG)
            m_old = m_ref[g]
            m_new = jnp.maximum(m_old, jnp.max(sc, axis=1, keepdims=True))
            p = jnp.exp2(sc - (jnp.concatenate([m_new] * nt, axis=1) if nt > 1 else m_new))
            acc_ref[g] = jnp.exp2(m_old - m_new) * acc_ref[g] + _dot(_bf(p), vt)
            m_ref[g] = m_new

    sel_chunk(i, 1, True, causal=True)
    near_quads = (N_TOK_TILES - 1) // 4

    def near_body(c, carry):
        sel_chunk(i - 4 * (c + 1), 4, True)
        return carry

    lax.fori_loop(0, jnp.minimum(near_quads, i // 4), near_body, 0)

    def far_body(c, carry):
        sel_chunk(i - 4 * (c + 1), 4, False)
        return carry

    lax.fori_loop(near_quads, i // 4, far_body, 0)

    @pl.when(i % 4 >= 2)
    def _():
        sel_chunk(i % 4 - 2, 2, True)

    @pl.when(i % 2 == 1)
    def _():
        sel_chunk(0, 1, True)

    for g in range(G_NSA):
        acc = acc_ref[g]
        den = acc[:, 0:1] if g else acc[:, LANES - 1:LANES]
        o_s.append(acc / den)

    for g in range(G_NSA):
        s_tiles, v_tiles, m_tiles = [], [], []
        for dlt in range(WINDOW // LANES + 1):
            kv = win_refs[dlt][0]
            kt = _bf(kv[:, 0:LANES])
            v_tiles.append(_bf(kv[:, LANES:2 * LANES]))
            dist = q_pos - ((i - dlt) * LANES + lane)
            mk = (dist >= 0) & (dist < WINDOW) & (i - dlt >= 0)
            bias = jnp.concatenate([tt_ref[dlt, g * R_NSA + r] for r in range(R_NSA)], axis=0)
            s_tiles.append(_dot_t(qg_ref[g], kt) + bias)
            m_tiles.append(jnp.concatenate([mk] * R_NSA, axis=0))
        p_w = _masked_softmax(jnp.concatenate(s_tiles, axis=1), jnp.concatenate(m_tiles, axis=1))
        o_w.append(_dot(_bf(p_w), jnp.concatenate(v_tiles, axis=0)))

    sig = jax.nn.sigmoid(gate_ref[0])
    o_ref[0] = _gate_combine(sig, o_c, o_s, o_w, tq)


def _nsa_prompt(proj, kcvc, tt, tts, tc, sel_map):
    bsz, t, _ = proj.shape
    tq = TQ
    ncp = t // CMP_STRIDE
    n_blk = t // SEL_BLOCK
    map_t = sel_map.T
    nbp = map_t.shape[0]
    rows = R_NSA * tq
    nwin = WINDOW // LANES + 1
    win_specs = [
        pl.BlockSpec((1, tq, 2 * LANES),
                     functools.partial(lambda b, i, d: (b, jnp.maximum(i - d, 0), (C_KVB + 4 * LANES) // (2 * LANES)), d=d))
        for d in range(nwin)]
    far = N_CMP_TILES - 1
    return pl.pallas_call(
        functools.partial(_nsa_prompt_kernel, ncp=ncp, n_blk=n_blk),
        grid=(bsz, t // tq),
        in_specs=[pl.BlockSpec((1, tq, W_NSA), lambda b, i: (b, i, C_QB // W_NSA)),
                  pl.BlockSpec((1, tq, LANES), lambda b, i: (b, i, C_GATE // LANES)),
                  pl.BlockSpec((1, ncp, 2 * LANES), lambda b, i: (b, 0, 0)),
                  pl.BlockSpec((1, t, 2 * LANES), lambda b, i: (b, 0, (C_KVB + 2 * LANES) // (2 * LANES)))]
                 + win_specs +
                 [pl.BlockSpec((nwin, H_NSA, tq, LANES), lambda b, i: (0, 0, 0, 0)),
                  pl.BlockSpec((N_TOK_TILES, H_NSA, tq, LANES), lambda b, i: (0, 0, 0, 0)),
                  pl.BlockSpec((1, H_NSA, tq, LANES), lambda b, i: (i % 16, 0, 0, 0)),
                  pl.BlockSpec((1, H_NSA, tq, LANES), lambda b, i: (jnp.minimum(i % 16 + 16, far), 0, 0, 0)),
                  pl.BlockSpec((1, H_NSA, tq, LANES), lambda b, i: (far, 0, 0, 0)),
                  pl.BlockSpec((nbp, ncp), lambda b, i: (0, 0))],
        out_specs=pl.BlockSpec((1, tq, W_NSA), lambda b, i: (b, i, 0)),
        out_shape=jax.ShapeDtypeStruct((bsz, t, W_NSA), F32),
        scratch_shapes=[pltpu.VMEM((G_NSA, rows, LANES), BF16),
                        pltpu.VMEM((G_NSA, rows, LANES + nbp), BF16),
                        pltpu.VMEM((G_NSA, rows, LANES), F32),
                        pltpu.VMEM((G_NSA, rows, LANES), F32)],
        compiler_params=_cparams(("parallel", "arbitrary")),
    )(proj, proj, kcvc, proj, *([proj] * nwin), tt[:nwin], tts, tc, tc, tc, map_t)


def _nsa_sample_kernel(pt_ref, p_ref, win_ref, cache_ref, w1_ref, pos_ref, w2_ref, map_ref,
                       bsel_ref, bcmp_ref, bwin_ref, o_ref, buf_t, sem, rowbuf, tail_ref, wbuf,
                       *, layer_base, n_pages, nq, n_blk):
    b = pl.program_id(0)
    past = n_pages * PAGE_SIZE
    tk = past + LANES
    ncp = past // CMP_STRIDE
    nbp = map_ref.shape[0]
    rows = R_NSA * nq
    all_rows = G_NSA * rows
    wlen = win_ref.shape[1]
    wpad = wbuf.shape[0]

    def page_copy(j):
        return pltpu.make_async_copy(
            cache_ref.at[layer_base + pt_ref[b, j]],
            buf_t.at[:, :, pl.ds(pl.multiple_of(j * PAGE_SIZE, PAGE_SIZE), PAGE_SIZE)], sem.at[j])

    def start(j, c):
        page_copy(j).start()
        return c

    lax.fori_loop(0, n_pages, start, 0)

    new_nsa = p_ref[0, :, C_KVB - C_QB:C_KVB - C_QB + 4 * LANES]
    zpad = jnp.zeros((LANES - nq, LANES), F32)
    for c in (2, 3):
        buf_t[c, :, past:tk] = jnp.concatenate([new_nsa[:, c * LANES:(c + 1) * LANES], zpad], axis=0).T
    wbuf[0:wlen, :] = win_ref[0]
    wbuf[wlen:wlen + nq, :] = p_ref[0, :, C_KVB - C_QB + 4 * LANES:C_KVB - C_QB + 6 * LANES]
    wbuf[wlen + nq:wpad, :] = jnp.zeros((wpad - wlen - nq, 2 * LANES), F32)

    q = p_ref[0, :, 0:W_NSA] * QK_SCALE
    sig = jax.nn.sigmoid(p_ref[0, :, C_GATE - C_QB:C_GATE - C_QB + LANES])
    q_all = jnp.concatenate([_group_queries(q, g, nq) for g in range(G_NSA)], axis=0)
    q_bf = _bf(q_all)

    kk = lax.broadcasted_iota(jnp.int32, (all_rows, wpad), 1)
    dist_w = wlen + (lax.broadcasted_iota(jnp.int32, (all_rows, wpad), 0) % nq) - kk
    mask_w = (dist_w >= 0) & (dist_w < WINDOW) & (kk < wlen + nq)
    p_w = _masked_softmax(_dot_t(q_bf, _bf(wbuf[:, 0:LANES])) + bwin_ref[...], mask_w)
    o_w = _dot(_bf(p_w), _bf(wbuf[:, LANES:2 * LANES]))

    group = max(n_pages // 4, 1)
    for j in range(n_pages):
        if j % group == 0:
            for jj in range(j, min(j + group, n_pages)):
                page_copy(jj).wait()
        for c in range(2):
            rowbuf[c, j * PAGE_SIZE:(j + 1) * PAGE_SIZE, :] = buf_t[c, :, j * PAGE_SIZE:(j + 1) * PAGE_SIZE].T
    kc, vc = _compress_rows((rowbuf.at[0], rowbuf.at[1]), ncp, w1_ref, pos_ref, w2_ref, tail_ref)

    n_idx = lax.broadcasted_iota(jnp.int32, (all_rows, ncp), 1)
    p_c = _masked_softmax(_dot3_t(q_all, kc) + bcmp_ref[...], n_idx < ncp - 1)
    o_c = _dot(_bf(p_c), _bf(vc))
    p_sum = []
    for g in range(G_NSA):
        acc = p_c[g * rows:g * rows + nq]
        for r in range(1, R_NSA):
            acc = acc + p_c[g * rows + r * nq:g * rows + (r + 1) * nq]
        p_sum.append(acc)
    p_hi, p_lo = _split(jnp.concatenate(p_sum + [jnp.zeros((LANES - G_NSA * nq, ncp), F32)], axis=0))
    imp_t = _dot_t(map_ref[...], p_hi) + _dot_t(map_ref[...], p_lo)
    q_pos_t = past + lax.broadcasted_iota(jnp.int32, (nbp, LANES), 1) % nq
    sel = _select_blocks(imp_t, q_pos_t, n_blk, axis=0).T
    sel_rows = jnp.concatenate([sel[g * nq:(g + 1) * nq] for g in range(G_NSA) for _ in range(R_NSA)], axis=0)

    tok = lax.broadcasted_iota(jnp.int32, (all_rows, tk), 1)
    qi_tok = lax.broadcasted_iota(jnp.int32, (all_rows, tk), 0) % nq
    blk_row = lax.broadcasted_iota(jnp.int32, (nbp, tk), 0)
    blk_tok = lax.broadcasted_iota(jnp.int32, (nbp, tk), 1) // SEL_BLOCK
    expand = jnp.where(blk_row == blk_tok, 1.0, 0.0).astype(BF16)
    mk = jnp.logical_and(_dot(_bf(sel_rows), expand) > 0.5, tok <= past + qi_tok)
    p_s = _masked_softmax(_dot(q_bf, _bf(buf_t[2])) + bsel_ref[...], mk)
    o_s = _dot_t(_bf(p_s), _bf(buf_t[3]))

    per_group = lambda o: [o[g * rows:(g + 1) * rows] for g in range(G_NSA)]
    o_ref[0] = _gate_combine(sig, per_group(o_c), per_group(o_s), per_group(o_w), nq)


def _nsa_sample(proj_s, win_cache, cache_pages, page_table, w1bd, posrow, w2bd, sel_map,
                bsel, bcmp, bwin, layer, nq):
    dbsz, n_pages = page_table.shape
    n_pool = cache_pages.shape[0] // (win_cache.shape[0] // dbsz)
    past = n_pages * PAGE_SIZE
    tk = past + LANES
    ncp = past // CMP_STRIDE
    wlen = win_cache.shape[1]
    wpad = wlen + LANES
    n_blk = -(-(past + nq) // SEL_BLOCK)
    kern = functools.partial(_nsa_sample_kernel, layer_base=layer * n_pool, n_pages=n_pages,
                             nq=nq, n_blk=n_blk)
    const2 = lambda b, pt: (0, 0)
    grid_spec = pltpu.PrefetchScalarGridSpec(
        num_scalar_prefetch=1,
        grid=(dbsz,),
        in_specs=[pl.BlockSpec((1, nq, N_PROJ - C_QB), lambda b, pt: (0, b, 1)),
                  pl.BlockSpec((1, wlen, 2 * LANES), lambda b, pt: (layer * dbsz + b, 0, 0)),
                  pl.BlockSpec(memory_space=pl.ANY),
                  pl.BlockSpec((CMP_STRIDE // 2, 2, 2 * LANES, 2 * LANES), lambda b, pt: (0, 0, 0, 0)),
                  pl.BlockSpec((2, 1, 2 * LANES), lambda b, pt: (0, 0, 0)),
                  pl.BlockSpec((2, LANES, LANES), lambda b, pt: (0, 0, 0)),
                  pl.BlockSpec(sel_map.shape, const2),
                  pl.BlockSpec(bsel.shape, const2),
                  pl.BlockSpec(bcmp.shape, const2),
                  pl.BlockSpec(bwin.shape, const2)],
        out_specs=pl.BlockSpec((1, nq, W_NSA), lambda b, pt: (0, b, 0)),
        scratch_shapes=[pltpu.VMEM((4, LANES, tk), F32),
                        pltpu.SemaphoreType.DMA((n_pages,)),
                        pltpu.VMEM((2, past, LANES), F32),
                        pltpu.VMEM((2, ncp + SUBLANES, LANES), F32),
                        pltpu.VMEM((wpad, 2 * LANES), F32)],
    )
    return pl.pallas_call(
        kern, grid_spec=grid_spec,
        out_shape=jax.ShapeDtypeStruct((1, dbsz * nq, W_NSA), F32),
        compiler_params=_cparams(("arbitrary",)),
    )(page_table, proj_s, win_cache, cache_pages, w1bd, posrow, w2bd, sel_map, bsel, bcmp, bwin)


def _out_kernel(y_ref, oa_ref, ob_ref, za_ref, zb_ref, gate_ref, norm_ref, w_ref, g_ref, b_ref, o_ref,
                *, alpha):
    def gated_norm(o, z, nrm):
        ms = jnp.mean(o * o, axis=-1, keepdims=True)
        return o * lax.rsqrt(ms + RMS_EPS) * nrm * (z * jax.nn.sigmoid(z))

    ua = gated_norm(oa_ref[0], za_ref[0], norm_ref[:, 0:W_SB])
    ub = gated_norm(ob_ref[0], zb_ref[0], norm_ref[:, W_SB:D_MODEL])
    u = _bf(jnp.concatenate([ua, ub], axis=1))
    mixed = _dot(u, w_ref[...])
    x = alpha * y_ref[0] + gate_ref[0] * mixed
    mu = jnp.mean(x, axis=-1, keepdims=True)
    xc = x - mu
    var = jnp.mean(xc * xc, axis=-1, keepdims=True)
    o_ref[0] = xc * lax.rsqrt(var + LN_EPS) * g_ref[...] + b_ref[...]


def _out_stage(y, o_a, o_b, proj, gate, norm_p, w_out_bf, ln_g, ln_b, alpha):
    bsz, t, d = y.shape
    tm = min(256, t)
    if gate.shape[1] == 1:
        gate_spec = pl.BlockSpec((1, 1, d), lambda b, i: (b, 0, 0))
    else:
        gate_spec = pl.BlockSpec((1, tm, d), lambda b, i: (b, i, 0))
    row = lambda w: pl.BlockSpec((1, w), lambda b, i: (0, 0))
    return pl.pallas_call(
        functools.partial(_out_kernel, alpha=alpha),
        grid=(bsz, t // tm),
        in_specs=[pl.BlockSpec((1, tm, d), lambda b, i: (b, i, 0)),
                  pl.BlockSpec((1, tm, W_SB), lambda b, i: (b, i, 0)),
                  pl.BlockSpec((1, tm, W_NSA), lambda b, i: (b, i, 0)),
                  pl.BlockSpec((1, tm, W_SB), lambda b, i: (b, i, C_ZA // W_SB)),
                  pl.BlockSpec((1, tm, W_NSA), lambda b, i: (b, i, C_ZB // W_NSA)),
                  gate_spec, row(d),
                  pl.BlockSpec((d, d), lambda b, i: (0, 0)),
                  row(d), row(d)],
        out_specs=pl.BlockSpec((1, tm, d), lambda b, i: (b, i, 0)),
        out_shape=jax.ShapeDtypeStruct((bsz, t, d), F32),
        compiler_params=_cparams(("parallel", "parallel")),
    )(y, o_a, o_b, proj, proj, gate, norm_p.reshape(1, d), w_out_bf, ln_g.reshape(1, d), ln_b.reshape(1, d))


def _rel_bucket(dist):
    n = jnp.maximum(dist, 0)
    nf = jnp.maximum(n, 1).astype(F32)
    large = REL_MAX_EXACT + (jnp.log(nf / REL_MAX_EXACT) / math.log(REL_MAX_DIST / REL_MAX_EXACT)
                             * (N_BUCKETS - REL_MAX_EXACT)).astype(jnp.int32)
    large = jnp.minimum(large, N_BUCKETS - 1)
    return jnp.where(n < REL_MAX_EXACT, n, large)


def _nsa_perm(x, axis):
    shp = x.shape
    x = x.reshape(shp[:axis] + (G_NSA, R_NSA, HEAD_DIM) + shp[axis + 1:])
    x = jnp.swapaxes(x, axis, axis + 1)
    return x.reshape(shp)


def _bias_tile_kernel(base_ref, step_ref, thr_ref, rb_ref, o_ref):
    t = pl.program_id(0)
    shape = o_ref.shape[2:]
    dist = (base_ref[t] + lax.broadcasted_iota(jnp.int32, shape, 0)
            - step_ref[t] * lax.broadcasted_iota(jnp.int32, shape, 1))
    for h in range(H_NSA):
        val = jnp.full(shape, rb_ref[0, h], F32)
        for b in range(1, N_BUCKETS):
            val = jnp.where(dist >= thr_ref[b], rb_ref[b, h], val)
        o_ref[0, h] = val


def _bias_tiles(base, step, thr, rel_bias, rows):
    n = len(base)
    grid_spec = pltpu.PrefetchScalarGridSpec(
        num_scalar_prefetch=3,
        grid=(n,),
        in_specs=[pl.BlockSpec(memory_space=pltpu.SMEM)],
        out_specs=pl.BlockSpec((1, H_NSA, rows, LANES), lambda t, *_: (t, 0, 0, 0)),
    )
    return pl.pallas_call(
        _bias_tile_kernel, grid_spec=grid_spec,
        out_shape=jax.ShapeDtypeStruct((n, H_NSA, rows, LANES), F32),
        compiler_params=_cparams(("parallel",)),
    )(jnp.asarray(base, jnp.int32), jnp.asarray(step, jnp.int32), thr, rel_bias)


def _bias_tables(rel_bias, nq, past, wlen):
    bucket = _rel_bucket(jnp.arange(BIAS_DMAX))
    thr = jnp.sum(bucket[None, :] < jnp.arange(N_BUCKETS)[:, None], axis=1).astype(jnp.int32)
    base = [LANES * d for d in range(N_TOK_TILES)] + [LANES * o - (CMP_LEN - 1) for o in range(N_CMP_TILES)]
    step = [1] * N_TOK_TILES + [CMP_STRIDE] * N_CMP_TILES
    tiles = _bias_tiles(base, step, thr, rel_bias, TQ)
    tt, tc = tiles[:N_TOK_TILES], tiles[N_TOK_TILES:]
    n_key, n_cmp, n_win = past // LANES + 1, past // (CMP_STRIDE * LANES), wlen // LANES + 1
    base = ([past - LANES * k for k in range(n_key)]
            + [past - CMP_STRIDE * LANES * j - (CMP_LEN - 1) for j in range(n_cmp)]
            + [wlen - LANES * k for k in range(n_win)])
    step = [1] * n_key + [CMP_STRIDE] * n_cmp + [1] * n_win
    tiles = _bias_tiles(base, step, thr, rel_bias, nq)
    rows = lambda x: jnp.transpose(x, (1, 2, 0, 3)).reshape(H_NSA * nq, x.shape[0] * LANES)
    bsel, bcmp, bwin = rows(tiles[:n_key]), rows(tiles[n_key:n_key + n_cmp]), rows(tiles[n_key + n_cmp:])
    return tt, tc, bsel, bcmp, bwin


def _selection_map(ncp, n_c, n_blk):
    nbp = -(-n_blk // LANES) * LANES
    cs = np.arange(ncp)[:, None] * CMP_STRIDE
    ss = np.arange(nbp)[None, :] * SEL_BLOCK
    ov = np.minimum(cs + CMP_LEN, ss + SEL_BLOCK) - np.maximum(cs, ss)
    m = np.clip(ov, 0, None).astype(np.float32) / CMP_LEN
    m = m * (np.arange(ncp)[:, None] < n_c) * (np.arange(nbp)[None, :] < n_blk)
    return jnp.asarray(m, dtype=BF16)


def _block_diag2(w):
    z = jnp.zeros_like(w)
    return jnp.concatenate([jnp.concatenate([w, z], axis=-1), jnp.concatenate([z, w], axis=-1)], axis=-2)


def kernel(x_prompt, x_sample, cache_sb_kv, cache_nsa_kv, cache_win_kv, page_table, c_prompt, c_sample,
           ln_in_g, ln_in_b, w_ada, b_ada, w_in, w_cmp1, w_cmp2, pos_cmp, norm_grp, w_out, ln_g, ln_b,
           rel_bias):
    bsz, seq, d = x_prompt.shape
    dbsz, nq, _ = x_sample.shape
    depth = w_ada.shape[0]
    n_pool = cache_sb_kv.shape[1]
    n_pages = page_table.shape[1]
    past = n_pages * PAGE_SIZE
    wlen = cache_win_kv.shape[2]
    alpha = (2 * depth) ** 0.25
    assert d == D_MODEL and seq % (16 * TQ) == 0 and past % (16 * TQ) == 0 and wlen == WINDOW
    assert seq // SEL_BLOCK >= SEL_TOPK and nq == SUBLANES

    w_qb = _nsa_perm(w_in[:, :, 2048:2560], 2)
    w_zb = _nsa_perm(w_in[:, :, 3352:3864], 2)
    pad = lambda n: jnp.zeros((depth, d, n), F32)
    w_in_p = _bf(jnp.concatenate(
        [w_in[:, :, 0:2048], w_qb, w_in[:, :, 2560:3328], w_in[:, :, 3328:3352], pad(C_ZB - C_GATE - 3 * H_NSA),
         w_zb], axis=2))
    norm_p = jnp.concatenate([norm_grp[:, :W_SB], _nsa_perm(norm_grp[:, W_SB:], 1)], axis=1)
    w_out_p = _bf(jnp.concatenate([w_out[:, :W_SB], _nsa_perm(w_out[:, W_SB:], 1)], axis=1))
    w1b = jnp.swapaxes(_block_diag2(w_cmp1), 1, 2)

    def pair_rows(w):
        w = jnp.swapaxes(w.reshape(depth, CMP_STRIDE // 2, 2, 2, LANES, LANES), 2, 3)
        return w.reshape(depth, CMP_STRIDE // 2, 2, 2 * LANES, LANES)

    w1bd = _bf(jnp.concatenate([pair_rows(w1b[:, :CMP_STRIDE]), pair_rows(w1b[:, CMP_STRIDE:])], axis=-1))
    w2bd = _bf(_block_diag2(w_cmp2))
    pos2 = jnp.einsum('dcxlk,dcxlkh->dcxh', pos_cmp.reshape(depth, 2, 2, CMP_STRIDE, HEAD_DIM),
                      w_cmp1.reshape(depth, 2, 2, CMP_STRIDE, HEAD_DIM, HEAD_DIM),
                      precision=lax.Precision.HIGHEST)
    posrow = jnp.concatenate([pos2[:, :, 0], pos2[:, :, 0], pos2[:, :, 1], pos2[:, :, 1]],
                             axis=-1)[:, :, None, :]
    tt, tc, bsel, bcmp, bwin = _bias_tables(rel_bias, nq, past, wlen)
    tts = (tt - tt[N_TOK_TILES - 1:]) * LOG2E
    map_p = _selection_map(seq // CMP_STRIDE, (seq - CMP_LEN) // CMP_STRIDE + 1, seq // SEL_BLOCK)
    n_blk_s = -(-(past + nq) // SEL_BLOCK)
    map_s = _selection_map(past // CMP_STRIDE, (past + nq - CMP_LEN) // CMP_STRIDE + 1, n_blk_s).T

    n_c = bsz + dbsz
    c_rows = -(-n_c // SUBLANES) * SUBLANES
    c_all = jnp.concatenate([c_prompt, c_sample, jnp.zeros((c_rows - n_c, d), F32)], axis=0)
    mod = _ada_mod(c_all, w_ada, b_ada)

    y_p = _layer_norm(x_prompt.reshape(bsz * seq, d), ln_in_g, ln_in_b).reshape(bsz, seq, d)
    y_s = _layer_norm(x_sample.reshape(dbsz * nq, d), ln_in_g, ln_in_b).reshape(1, dbsz * nq, d)

    sb_pages = jnp.transpose(cache_sb_kv, (0, 1, 3, 4, 5, 2)).reshape(depth * n_pool, 2, H_SB, HEAD_DIM, PAGE_SIZE)
    nsa_pages = jnp.transpose(cache_nsa_kv, (0, 1, 3, 4, 5, 2)).reshape(depth * n_pool, 4, KV_NSA, PAGE_SIZE)
    win_cache = cache_win_kv.reshape(depth * dbsz, wlen, 2 * KV_NSA)

    p_sb, p_nsa, p_win, s_sb, s_nsa, s_win = [], [], [], [], [], []
    for l in range(depth):
        mp = mod[l, :bsz].reshape(bsz, 1, 3 * d)
        proj = _in_proj(y_p, mp[:, :, 0:d], mp[:, :, d:2 * d], w_in_p[l])
        o_a = _sb_prompt(proj)
        kcvc = _compress_prompt(proj, w1bd[l], posrow[l], w2bd[l])
        o_b = _nsa_prompt(proj, kcvc, tt, tts, tc, map_p)
        p_sb.append(proj[:, :, C_KA:C_ZA].reshape(bsz, seq, 2, H_SB, HEAD_DIM))
        p_nsa.append(proj[:, :, C_KVB:C_KVB + 4 * KV_NSA].reshape(bsz, seq, 4, G_NSA, HEAD_DIM))
        p_win.append(proj[:, seq - min(WINDOW, seq):, C_KVB + 4 * KV_NSA:C_KVB + 6 * KV_NSA]
                     .reshape(bsz, min(WINDOW, seq), 2, G_NSA, HEAD_DIM))
        y_p = _out_stage(y_p, o_a, o_b, proj, mp[:, :, 2 * d:3 * d], norm_p[l], w_out_p[l], ln_g[l], ln_b[l],
                         alpha)
        ms = jnp.repeat(mod[l, bsz:bsz + dbsz], nq, axis=0).reshape(1, dbsz * nq, 3 * d)
        proj_s = _in_proj(y_s, ms[:, :, 0:d], ms[:, :, d:2 * d], w_in_p[l])
        o_a = _sb_sample(proj_s, sb_pages, page_table, l * n_pool, nq)
        o_b = _nsa_sample(proj_s, win_cache, nsa_pages, page_table, w1bd[l], posrow[l], w2bd[l], map_s,
                          bsel, bcmp, bwin, l, nq)
        ps = proj_s.reshape(dbsz, nq, N_PROJ)
        s_sb.append(ps[:, :, C_KA:C_ZA].reshape(dbsz, nq, 2, H_SB, HEAD_DIM))
        s_nsa.append(ps[:, :, C_KVB:C_KVB + 4 * KV_NSA].reshape(dbsz, nq, 4, G_NSA, HEAD_DIM))
        new_win = ps[:, :, C_KVB + 4 * KV_NSA:C_KVB + 6 * KV_NSA].reshape(dbsz, nq, 2, G_NSA, HEAD_DIM)
        win_all = jnp.concatenate([cache_win_kv[l], new_win], axis=1)
        s_win.append(win_all[:, win_all.shape[1] - min(WINDOW, past + nq):])
        y_s = _out_stage(y_s, o_a, o_b, proj_s, ms[:, :, 2 * d:3 * d], norm_p[l], w_out_p[l], ln_g[l],
                         ln_b[l], alpha)

    return (y_p, y_s.reshape(dbsz, nq, d), jnp.stack(p_sb), jnp.stack(p_nsa), jnp.stack(p_win),
            jnp.stack(s_sb), jnp.stack(s_nsa), jnp.stack(s_win))
```
